```python
import math
import jax
import jax.numpy as jnp
from jax import lax
import numpy as np

D_MODEL = 4096
BATCH = 1
SEQ = 16384
DEPTH = 2

CHUNK = 64
N_BRANCH = 4
D_MIX = D_MODEL // 4
RET_HEADS = 4
RET_DK = D_MIX // RET_HEADS
RET_DV = D_MIX // RET_HEADS
ROPE_BASE = 10000.0
LRU_BLOCKS = 8
LRU_BLOCK = D_MIX // LRU_BLOCKS
LRU_C = 8.0
CONV_WIDTH = 4
HG_HEADS = 8
HG_DK = D_MIX // HG_HEADS
HG_DV = D_MIX // HG_HEADS
DN_HEADS = 8
DN_DK = D_MIX // DN_HEADS
DN_DV = D_MIX // DN_HEADS
FFN_HIDDEN = (((8 * D_MODEL + 2) // 3 + 255) // 256) * 256
N_IN = 14 * D_MIX + 2 * DN_HEADS + N_BRANCH * D_MODEL
EPS = 1e-6

kernel_name = 'hybrid_stream_retention_rglru_hgrn2_gdn'


def _f32(t):
    return t.astype(jnp.float32)


def rmsnorm(x, g):
    x32 = _f32(x)
    y = x32 * lax.rsqrt(jnp.mean(x32 * x32, axis=-1, keepdims=True) + EPS)
    return (y * _f32(g)).astype(x.dtype)


def causal_depthwise_conv(x, w, b=None):
    c = x.shape[-1]
    y = lax.conv_general_dilated(x, w[:, None, :], window_strides=(1,), padding=[(CONV_WIDTH - 1, 0)],
                                 dimension_numbers=('NWC', 'WIO', 'NWC'), feature_group_count=c)
    if b is not None:
        y = y + b
    return y


def to_chunks(t):
    b, s, h, d = t.shape
    return t.reshape(b, s // CHUNK, CHUNK, h, d).transpose(0, 3, 1, 2, 4)


def from_chunks(t):
    b, h, n, c, d = t.shape
    return t.transpose(0, 2, 3, 1, 4).reshape(b, n * c, h, d)


def rotary(t):
    s, d = t.shape[1], t.shape[-1]
    half = d // 2
    inv_freq = 1.0 / (ROPE_BASE ** (jnp.arange(half, dtype=jnp.float32) / half))
    ang = jnp.arange(s, dtype=jnp.float32)[:, None] * inv_freq[None, :]
    cos = jnp.cos(ang)[None, :, None, :]
    sin = jnp.sin(ang)[None, :, None, :]
    t1, t2 = t[..., :half], t[..., half:]
    return jnp.concatenate([t1 * cos - t2 * sin, t1 * sin + t2 * cos], axis=-1)


def split_columns(u):
    sizes = [D_MIX] * 14 + [DN_HEADS, DN_HEADS, N_BRANCH * D_MODEL]
    parts = []
    off = 0
    for size in sizes:
        parts.append(u[..., off:off + size])
        off += size
    return parts


def retention_mixer(q, k, v, g, gn_w):
    b, s, _ = q.shape
    q = rotary(_f32(q).reshape(b, s, RET_HEADS, RET_DK)) * RET_DK ** -0.5
    k = rotary(_f32(k).reshape(b, s, RET_HEADS, RET_DK))
    v = _f32(v).reshape(b, s, RET_HEADS, RET_DV)
    log_gamma = jnp.log(1.0 - 2.0 ** (-5.0 - jnp.arange(RET_HEADS, dtype=jnp.float32)))
    pos = jnp.arange(CHUNK, dtype=jnp.float32)
    intra_decay = jnp.exp(log_gamma[:, None, None] * jnp.abs(pos[:, None] - pos[None, :]))
    q_decay = jnp.exp(log_gamma[:, None] * pos[None, :])[:, :, None]
    k_decay = jnp.exp(log_gamma[:, None] * (CHUNK - pos)[None, :])[:, :, None]
    chunk_decay = jnp.exp(log_gamma * CHUNK)[:, None, None]
    qc, kc, vc = to_chunks(q), to_chunks(k), to_chunks(v)
    scores = jnp.einsum('bhnid,bhnjd->bhnij', qc, kc) * intra_decay[:, None]
    o_intra = jnp.einsum('bhnij,bhnjd->bhnid', scores, vc)

    def step(state, inp):
        qn, kn, vn = inp
        o = jnp.einsum('bhid,bhde->bhie', qn * q_decay, state)
        state = chunk_decay * state + jnp.einsum('bhjd,bhje->bhde', kn * k_decay, vn)
        return state, o

    state0 = jnp.zeros((b, RET_HEADS, RET_DK, RET_DV), jnp.float32)
    _, o_inter = lax.scan(step, state0, (jnp.moveaxis(qc, 2, 0), jnp.moveaxis(kc, 2, 0), jnp.moveaxis(vc, 2, 0)))
    o = from_chunks(o_intra + jnp.moveaxis(o_inter, 0, 2))
    mu = jnp.mean(o, axis=-1, keepdims=True)
    var = jnp.mean(jnp.square(o - mu), axis=-1, keepdims=True)
    o = ((o - mu) * lax.rsqrt(var + EPS)).reshape(b, s, D_MIX) * _f32(gn_w)
    return jax.nn.silu(_f32(g)) * o


def rglru_mixer(xb, gate, conv_w, conv_b, w_r, b_r, w_i, b_i, a_param):
    b, s, _ = xb.shape
    xc = causal_depthwise_conv(_f32(xb), _f32(conv_w), _f32(conv_b))
    xblk = xc.reshape(b, s, LRU_BLOCKS, LRU_BLOCK)
    r = jax.nn.sigmoid(jnp.einsum('bsnk,nkl->bsnl', xblk, _f32(w_r)).reshape(b, s, D_MIX) + _f32(b_r))
    i = jax.nn.sigmoid(jnp.einsum('bsnk,nkl->bsnl', xblk, _f32(w_i)).reshape(b, s, D_MIX) + _f32(b_i))
    log_a = -LRU_C * r * jax.nn.softplus(-_f32(a_param))
    a = jnp.exp(log_a)
    bx = jnp.sqrt(-jnp.expm1(2.0 * log_a)) * (i * xc)

    def combine(c1, c2):
        a1, b1 = c1
        a2, b2 = c2
        return a1 * a2, a2 * b1 + b2

    _, h = lax.associative_scan(combine, (a, bx), axis=1)
    return h * jax.nn.gelu(_f32(gate))


def hgrn2_mixer(q, f_logit, i, g, lb, norm_w):
    b, s, _ = q.shape
    lb = _f32(lb)
    f = lb + (1.0 - lb) * jax.nn.sigmoid(_f32(f_logit))
    log_f = jnp.log(f).reshape(b, s, HG_HEADS, HG_DK)
    k = (1.0 - f).reshape(b, s, HG_HEADS, HG_DK)
    q = jax.nn.silu(_f32(q)).reshape(b, s, HG_HEADS, HG_DK) * HG_DK ** -0.5
    v = _f32(i).reshape(b, s, HG_HEADS, HG_DV)
    causal = jnp.tril(jnp.ones((CHUNK, CHUNK), dtype=bool))

    def step(state, inp):
        qn, kn, vn, lf = inp
        cum = jnp.cumsum(lf, axis=2)
        o_inter = jnp.einsum('bhid,bhde->bhie', qn * jnp.exp(cum), state)
        diff = cum[:, :, :, None, :] - cum[:, :, None, :, :]
        decay = jnp.exp(jnp.where(causal[:, :, None], diff, -jnp.inf))
        attn = jnp.einsum('bhijd,bhjd->bhij', qn[:, :, :, None, :] * decay, kn)
        o = o_inter + jnp.einsum('bhij,bhje->bhie', attn, vn)
        last = cum[:, :, -1:, :]
        state = jnp.exp(last[:, :, 0, :])[..., None] * state + jnp.einsum('bhjd,bhje->bhde', kn * jnp.exp(last - cum), vn)
        return state, o

    state0 = jnp.zeros((b, HG_HEADS, HG_DK, HG_DV), jnp.float32)
    xs = (jnp.moveaxis(to_chunks(q), 2, 0), jnp.moveaxis(to_chunks(k), 2, 0),
          jnp.moveaxis(to_chunks(v), 2, 0), jnp.moveaxis(to_chunks(log_f), 2, 0))
    _, o = lax.scan(step, state0, xs)
    o = from_chunks(jnp.moveaxis(o, 0, 2))
    o = (o * lax.rsqrt(jnp.mean(o * o, axis=-1, keepdims=True) + EPS)).reshape(b, s, D_MIX) * _f32(norm_w)
    return o * jax.nn.silu(_f32(g))


def gated_deltanet_mixer(q, k, v, z, a, beta_logit, conv_w, a_log, dt_bias, norm_w):
    b, s, _ = q.shape
    qkv = jax.nn.silu(causal_depthwise_conv(_f32(jnp.concatenate([q, k, v], axis=-1)), _f32(conv_w)))
    q, k, v = jnp.split(qkv, 3, axis=-1)

    def l2n(t):
        return t * lax.rsqrt(jnp.sum(t * t, axis=-1, keepdims=True) + EPS)

    q = l2n(q.reshape(b, s, DN_HEADS, DN_DK)) * DN_DK ** -0.5
    k = l2n(k.reshape(b, s, DN_HEADS, DN_DK))
    v = v.reshape(b, s, DN_HEADS, DN_DV)
    beta = jax.nn.sigmoid(_f32(beta_logit))
    log_alpha = -jnp.exp(_f32(a_log)) * jax.nn.softplus(_f32(a) + _f32(dt_bias))
    qc, kc, vc = to_chunks(q), to_chunks(k), to_chunks(v)
    betac = to_chunks(beta[..., None])[..., 0]
    g = jnp.cumsum(to_chunks(log_alpha[..., None])[..., 0], axis=-1)
    strict = jnp.tril(jnp.ones((CHUNK, CHUNK), dtype=bool), -1)
    incl = jnp.tril(jnp.ones((CHUNK, CHUNK), dtype=bool))
    decay = jnp.exp(jnp.where(incl, g[..., :, None] - g[..., None, :], -jnp.inf))
    kk = jnp.einsum('bhnid,bhnjd->bhnij', kc, kc)
    lower = jnp.where(strict, betac[..., None] * kk * decay, 0.0)
    eye = jnp.eye(CHUNK, dtype=jnp.float32)
    rhs = jnp.concatenate([vc * betac[..., None], kc * (betac * jnp.exp(g))[..., None]], axis=-1)
    sol = lax.linalg.triangular_solve(eye + lower, rhs, left_side=True, lower=True, unit_diagonal=True)
    u_val, w_key = sol[..., :DN_DV], sol[..., DN_DV:]
    qk = jnp.einsum('bhnid,bhnjd->bhnij', qc, kc) * decay
    q_g = qc * jnp.exp(g)[..., None]
    g_last = g[..., -1:]
    k_tail = kc * jnp.exp(g_last - g)[..., None]
    chunk_decay = jnp.exp(g_last)[..., None]

    def step(state, inp):
        u_n, w_n, qk_n, qg_n, kt_n, cd_n = inp
        v_new = u_n - jnp.einsum('bhcd,bhde->bhce', w_n, state)
        o = jnp.einsum('bhid,bhde->bhie', qg_n, state) + jnp.einsum('bhij,bhje->bhie', qk_n, v_new)
        state = cd_n * state + jnp.einsum('bhjd,bhje->bhde', kt_n, v_new)
        return state, o

    state0 = jnp.zeros((b, DN_HEADS, DN_DK, DN_DV), jnp.float32)
    xs = (jnp.moveaxis(u_val, 2, 0), jnp.moveaxis(w_key, 2, 0), jnp.moveaxis(qk, 2, 0),
          jnp.moveaxis(q_g, 2, 0), jnp.moveaxis(k_tail, 2, 0), jnp.moveaxis(chunk_decay, 2, 0))
    _, o = lax.scan(step, state0, xs)
    o = from_chunks(jnp.moveaxis(o, 0, 2))
    o = o * lax.rsqrt(jnp.mean(o * o, axis=-1, keepdims=True) + EPS) * _f32(norm_w)
    return o.reshape(b, s, D_MIX) * jax.nn.silu(_f32(z))


def setup_inputs(seed: int = 0) -> dict:
    key = jax.random.key(seed)
    ks = jax.random.split(key, 26)
    L = DEPTH

    def nrm(k, shape, scale):
        return jax.random.normal(k, shape, jnp.float32) * scale

    x = nrm(ks[0], (BATCH, SEQ, D_MODEL), 1.0)
    norm_mix = 1.0 + nrm(ks[1], (L, D_MODEL), 0.1)
    w_in = nrm(ks[2], (L, D_MODEL, N_IN), D_MODEL ** -0.5)
    merge_bias = nrm(ks[3], (L, N_BRANCH * D_MODEL), 0.02)
    ret_gn = 1.0 + nrm(ks[4], (L, D_MIX), 0.1)
    lru_conv_w = nrm(ks[5], (L, CONV_WIDTH, D_MIX), CONV_WIDTH ** -0.5)
    lru_conv_b = nrm(ks[6], (L, D_MIX), 0.02)
    lru_w_r = nrm(ks[7], (L, LRU_BLOCKS, LRU_BLOCK, LRU_BLOCK), LRU_BLOCK ** -0.5)
    lru_b_r = nrm(ks[8], (L, D_MIX), 0.02)
    lru_w_i = nrm(ks[9], (L, LRU_BLOCKS, LRU_BLOCK, LRU_BLOCK), LRU_BLOCK ** -0.5)
    lru_b_i = nrm(ks[10], (L, D_MIX), 0.02)
    a_pow = jax.random.uniform(ks[11], (L, D_MIX), dtype=jnp.float32, minval=0.9, maxval=0.999)
    p = a_pow ** (1.0 / LRU_C)
    lru_a = jnp.log(p) - jnp.log1p(-p)
    hg_lb_logits = nrm(ks[12], (L, D_MIX), 0.5)
    hg_norm = 1.0 + nrm(ks[13], (L, D_MIX), 0.1)
    dn_conv_w = nrm(ks[14], (L, CONV_WIDTH, 3 * D_MIX), CONV_WIDTH ** -0.5)
    dn_a_log = jnp.log(jax.random.uniform(ks[15], (L, DN_HEADS), dtype=jnp.float32, minval=1.0, maxval=16.0))
    dt = jnp.exp(jax.random.uniform(ks[16], (L, DN_HEADS), dtype=jnp.float32, minval=math.log(1e-3), maxval=math.log(1e-1)))
    dn_dt_bias = dt + jnp.log(-jnp.expm1(-dt))
    dn_norm = 1.0 + nrm(ks[17], (L, DN_DV), 0.1)
    w_branch = nrm(ks[18], (L, N_BRANCH, D_MIX, D_MODEL), D_MIX ** -0.5)
    w_out = nrm(ks[19], (L, D_MODEL, D_MODEL), D_MODEL ** -0.5)
    norm_ffn = 1.0 + nrm(ks[20], (L, D_MODEL), 0.1)
    w_ffn_gate = nrm(ks[21], (L, D_MODEL, FFN_HIDDEN), D_MODEL ** -0.5)
    w_ffn_up = nrm(ks[22], (L, D_MODEL, FFN_HIDDEN), D_MODEL ** -0.5)
    w_ffn_down = nrm(ks[23], (L, FFN_HIDDEN, D_MODEL), FFN_HIDDEN ** -0.5)
    norm_final = 1.0 + nrm(ks[24], (D_MODEL,), 0.1)
    return {'x': x, 'norm_mix': norm_mix, 'w_in': w_in, 'merge_bias': merge_bias, 'ret_gn': ret_gn,
            'lru_conv_w': lru_conv_w, 'lru_conv_b': lru_conv_b, 'lru_w_r': lru_w_r, 'lru_b_r': lru_b_r,
            'lru_w_i': lru_w_i, 'lru_b_i': lru_b_i, 'lru_a': lru_a, 'hg_lb_logits': hg_lb_logits,
            'hg_norm': hg_norm, 'dn_conv_w': dn_conv_w, 'dn_a_log': dn_a_log, 'dn_dt_bias': dn_dt_bias,
            'dn_norm': dn_norm, 'w_branch': w_branch, 'w_out': w_out, 'norm_ffn': norm_ffn,
            'w_ffn_gate': w_ffn_gate, 'w_ffn_up': w_ffn_up, 'w_ffn_down': w_ffn_down, 'norm_final': norm_final}


def reference(x, norm_mix, w_in, merge_bias, ret_gn, lru_conv_w, lru_conv_b, lru_w_r, lru_b_r, lru_w_i,
              lru_b_i, lru_a, hg_lb_logits, hg_norm, dn_conv_w, dn_a_log, dn_dt_bias, dn_norm, w_branch,
              w_out, norm_ffn, w_ffn_gate, w_ffn_up, w_ffn_down, norm_final):
    b, s, _ = x.shape
    lb_all = jnp.cumsum(jax.nn.softmax(_f32(hg_lb_logits), axis=0), axis=0)
    lb_all = lb_all - lb_all[0:1]
    for l in range(DEPTH):
        h = rmsnorm(x, norm_mix[l])
        u = h @ w_in[l]
        (r_q, r_k, r_v, r_g, l_x, l_gate, h_q, h_f, h_i, h_g,
         d_q, d_k, d_v, d_z, d_a, d_b, gate_logits) = split_columns(u)
        y_ret = retention_mixer(r_q, r_k, r_v, r_g, ret_gn[l])
        y_lru = rglru_mixer(l_x, l_gate, lru_conv_w[l], lru_conv_b[l], lru_w_r[l], lru_b_r[l],
                            lru_w_i[l], lru_b_i[l], lru_a[l])
        y_hg = hgrn2_mixer(h_q, h_f, h_i, h_g, lb_all[l], hg_norm[l])
        y_dn = gated_deltanet_mixer(d_q, d_k, d_v, d_z, d_a, d_b, dn_conv_w[l], dn_a_log[l],
                                    dn_dt_bias[l], dn_norm[l])
        gates = jax.nn.sigmoid(_f32(gate_logits) + _f32(merge_bias[l])).reshape(b, s, N_BRANCH, D_MODEL)
        branches = (y_ret, y_lru, y_hg, y_dn)
        merged = jnp.zeros((b, s, D_MODEL), jnp.float32)
        for n in range(N_BRANCH):
            merged = merged + gates[:, :, n, :] * _f32(branches[n].astype(x.dtype) @ w_branch[l, n])
        x = x + merged.astype(x.dtype) @ w_out[l]
        h2 = rmsnorm(x, norm_ffn[l])
        x = x + (jax.nn.silu(h2 @ w_ffn_gate[l]) * (h2 @ w_ffn_up[l])) @ w_ffn_down[l]
    return rmsnorm(x, norm_final)
```

```python
import functools
import math

import jax
import jax.numpy as jnp
from jax import lax
from jax.experimental import pallas as pl
from jax.experimental.pallas import tpu as pltpu

F32 = jnp.float32
BF16 = jnp.bfloat16

CHUNK = 64
SUB = 16
D_MIX = 1024
RET_HEADS = 4
RET_DK = 256
ROPE_BASE = 10000.0
LRU_BLOCKS = 8
LRU_BLOCK = 128
LRU_C = 8.0
CONV_WIDTH = 4
HG_HEADS = 8
HG_DK = 128
DN_HEADS = 8
DN_DK = 128
EPS = 1e-6
LANES = 128
TAIL = 8

V7X_VMEM_LIMIT = 56 * 1024 * 1024

_RET_LOG_GAMMA = tuple(math.log(1.0 - 2.0 ** (-5.0 - h)) for h in range(RET_HEADS))

NT = (((1,), (1,)), ((), ()))
TN = (((0,), (0,)), ((), ()))


def _dot(a, b):
    return jnp.dot(a, b, preferred_element_type=F32)


def _dot_nt(a, b):
    return lax.dot_general(a, b, NT, preferred_element_type=F32)


def _dot_tn(a, b):
    return lax.dot_general(a, b, TN, preferred_element_type=F32)


def _dot_f32(a, b):
    return jnp.dot(a, b, preferred_element_type=F32, precision=lax.Precision.HIGHEST)


def _bf(x):
    return x.astype(BF16)


def _silu(x):
    return x * jax.nn.sigmoid(x)


def _softplus(x):
    return jnp.maximum(x, 0.0) + jnp.log(1.0 + jnp.exp(-jnp.abs(x)))


def _params(sem, vmem=None):
    return pltpu.CompilerParams(dimension_semantics=sem, vmem_limit_bytes=vmem)


def _row_iota(shape):
    return lax.broadcasted_iota(jnp.int32, shape, 0)


def _col_iota(shape):
    return lax.broadcasted_iota(jnp.int32, shape, 1)


def _segment_cumsum(x, seg):
    pos = _row_iota(x.shape) % seg
    s = 1
    while s < seg:
        x = x + jnp.where(pos >= s, pltpu.roll(x, s, 0), 0.0)
        s *= 2
    return x


def _rmsnorm_kernel(x_ref, g_ref, o_ref):
    x = x_ref[...]
    y = x * lax.rsqrt(jnp.mean(x * x, axis=-1, keepdims=True) + EPS)
    o_ref[...] = (y * g_ref[...]).astype(o_ref.dtype)


def rmsnorm(x, g, out_dtype, tb=256):
    s, d = x.shape
    return pl.pallas_call(
        _rmsnorm_kernel,
        grid=(s // tb,),
        in_specs=[pl.BlockSpec((tb, d), lambda i: (i, 0)), pl.BlockSpec((1, d), lambda i: (0, 0))],
        out_specs=pl.BlockSpec((tb, d), lambda i: (i, 0)),
        out_shape=jax.ShapeDtypeStruct((s, d), out_dtype),
        compiler_params=_params(("arbitrary",)),
        name="rmsnorm",
    )(x, g.reshape(1, d))


def _mm_kernel(a_ref, b_ref, o_ref):
    o_ref[...] = _dot(a_ref[...], b_ref[...]).astype(o_ref.dtype)


def matmul(a, b, out_dtype, tm, tn, name):
    m, k = a.shape
    n = b.shape[1]
    tm, tn = min(tm, m), min(tn, n)
    return pl.pallas_call(
        _mm_kernel,
        grid=(m // tm, n // tn),
        in_specs=[pl.BlockSpec((tm, k), lambda i, j: (i, 0)), pl.BlockSpec((k, tn), lambda i, j: (0, j))],
        out_specs=pl.BlockSpec((tm, tn), lambda i, j: (i, j)),
        out_shape=jax.ShapeDtypeStruct((m, n), out_dtype),
        compiler_params=_params(("arbitrary", "arbitrary"), V7X_VMEM_LIMIT),
        name=name,
    )(a, b)


def _mm_res_kernel(a_ref, b_ref, r_ref, o_ref):
    @pl.when(pl.program_id(2) == 0)
    def _():
        o_ref[...] = r_ref[...]

    o_ref[...] += _dot(a_ref[...], b_ref[...])


def matmul_residual(a, b, res, tm, tn, tk, name):
    m, k = a.shape
    n = b.shape[1]
    tm, tn = min(tm, m), min(tn, n)
    return pl.pallas_call(
        _mm_res_kernel,
        grid=(m // tm, n // tn, k // tk),
        in_specs=[pl.BlockSpec((tm, tk), lambda i, j, kk: (i, kk)),
                  pl.BlockSpec((tk, tn), lambda i, j, kk: (kk, j)),
                  pl.BlockSpec((tm, tn), lambda i, j, kk: (i, j))],
        out_specs=pl.BlockSpec((tm, tn), lambda i, j, kk: (i, j)),
        out_shape=jax.ShapeDtypeStruct((m, n), F32),
        compiler_params=_params(("arbitrary", "arbitrary", "arbitrary"), V7X_VMEM_LIMIT),
        name=name,
    )(a, b, res)


def _ffn_up_kernel(h_ref, wg_ref, wu_ref, o_ref):
    h = h_ref[...]
    o_ref[...] = (_silu(_dot(h, wg_ref[...])) * _dot(h, wu_ref[...])).astype(o_ref.dtype)


def ffn_up(h, wg, wu, tm, tn):
    m, k = h.shape
    n = wg.shape[1]
    tm = min(tm, m)
    return pl.pallas_call(
        _ffn_up_kernel,
        grid=(m // tm, n // tn),
        in_specs=[pl.BlockSpec((tm, k), lambda i, j: (i, 0)),
                  pl.BlockSpec((k, tn), lambda i, j: (0, j)),
                  pl.BlockSpec((k, tn), lambda i, j: (0, j))],
        out_specs=pl.BlockSpec((tm, tn), lambda i, j: (i, j)),
        out_shape=jax.ShapeDtypeStruct((m, n), BF16),
        compiler_params=_params(("arbitrary", "arbitrary"), V7X_VMEM_LIMIT),
        name="ffn_up",
    )(h, wg, wu)


def _merge_kernel(g0, g1, g2, g3, b_ref, y0, y1, y2, y3, w_ref, o_ref):
    acc = None
    for n, (g_ref, y_ref) in enumerate(((g0, y0), (g1, y1), (g2, y2), (g3, y3))):
        gate = jax.nn.sigmoid(g_ref[...] + b_ref[n:n + 1, :])
        term = gate * _dot(y_ref[...], w_ref[n])
        acc = term if acc is None else acc + term
    o_ref[...] = acc.astype(o_ref.dtype)


def merge(gl, bias, ys, w_branch, tm, tn):
    s = gl.shape[0]
    nb, dmix, d = w_branch.shape
    tm = min(tm, s)
    nj = d // tn
    g_specs = [pl.BlockSpec((tm, tn), functools.partial(lambda i, j, n: (i, n * nj + j), n=n)) for n in range(nb)]
    y_specs = [pl.BlockSpec((tm, dmix), lambda i, j: (i, 0)) for _ in range(nb)]
    return pl.pallas_call(
        _merge_kernel,
        grid=(s // tm, nj),
        in_specs=g_specs + [pl.BlockSpec((nb, tn), lambda i, j: (0, j))] + y_specs
        + [pl.BlockSpec((nb, dmix, tn), lambda i, j: (0, 0, j))],
        out_specs=pl.BlockSpec((tm, tn), lambda i, j: (i, j)),
        out_shape=jax.ShapeDtypeStruct((s, d), BF16),
        compiler_params=_params(("arbitrary", "arbitrary"), V7X_VMEM_LIMIT),
        name="merge",
    )(gl, gl, gl, gl, bias.reshape(nb, d), *ys, w_branch)


def _retention_kernel(inv_ref, q_ref, k_ref, v_ref, g_ref, gn_ref, o_ref, state_ref, *, tb):
    i = pl.program_id(0)
    h = pl.program_id(1)

    @pl.when(i == 0)
    def _():
        state_ref[h] = jnp.zeros((RET_DK, RET_DK), F32)

    half = RET_DK // 2
    pos = (i * tb + _row_iota((tb, 1))).astype(F32)
    ang = pos * inv_ref[...]
    cos, sin = jnp.cos(ang), jnp.sin(ang)

    def rotary(t):
        t1, t2 = t[:, :half], t[:, half:]
        return jnp.concatenate([t1 * cos - t2 * sin, t1 * sin + t2 * cos], axis=-1)

    q = rotary(q_ref[...]) * RET_DK ** -0.5
    k = rotary(k_ref[...])
    v = v_ref[...]

    log_gamma = jnp.float32(_RET_LOG_GAMMA[0])
    for hh in range(1, RET_HEADS):
        log_gamma = jnp.where(h == hh, jnp.float32(_RET_LOG_GAMMA[hh]), log_gamma)
    ci = _row_iota((CHUNK, CHUNK))
    cj = _col_iota((CHUNK, CHUNK))
    intra = jnp.exp(log_gamma * jnp.abs(ci - cj).astype(F32))
    cpos = _row_iota((CHUNK, 1)).astype(F32)
    q_decay = jnp.exp(log_gamma * cpos)
    k_decay = jnp.exp(log_gamma * (CHUNK - cpos))
    chunk_decay = jnp.exp(jnp.full((1, 1), log_gamma * CHUNK, F32))

    state = state_ref[h]
    outs = []
    for c in range(tb // CHUNK):
        rows = slice(c * CHUNK, (c + 1) * CHUNK)
        qc, kc, vc = q[rows], k[rows], _bf(v[rows])
        scores = _dot_nt(_bf(qc), _bf(kc)) * intra
        o = _dot(_bf(scores), vc) + _dot(_bf(qc * q_decay), _bf(state))
        state = chunk_decay * state + _dot_tn(_bf(kc * k_decay), vc)
        outs.append(o)
    state_ref[h] = state
    o = jnp.concatenate(outs, axis=0)
    mu = jnp.mean(o, axis=-1, keepdims=True)
    var = jnp.mean(jnp.square(o - mu), axis=-1, keepdims=True)
    o = (o - mu) * lax.rsqrt(var + EPS) * gn_ref[...]
    o_ref[...] = (_silu(g_ref[...]) * o).astype(o_ref.dtype)


def retention(u, col0, gn_w, tb):
    s = u.shape[0]
    half = RET_DK // 2
    inv_freq = (1.0 / (ROPE_BASE ** (jnp.arange(half, dtype=F32) / half))).reshape(1, half)
    base = col0 // RET_DK
    per = D_MIX // RET_DK

    def col(part):
        return pl.BlockSpec((tb, RET_DK), lambda i, h: (i, base + part * per + h))

    return pl.pallas_call(
        functools.partial(_retention_kernel, tb=tb),
        grid=(s // tb, RET_HEADS),
        in_specs=[pl.BlockSpec((1, half), lambda i, h: (0, 0)), col(0), col(1), col(2), col(3),
                  pl.BlockSpec((1, RET_DK), lambda i, h: (0, h))],
        out_specs=pl.BlockSpec((tb, RET_DK), lambda i, h: (i, h)),
        out_shape=jax.ShapeDtypeStruct((s, D_MIX), BF16),
        scratch_shapes=[pltpu.VMEM((RET_HEADS, RET_DK, RET_DK), F32)],
        compiler_params=_params(("arbitrary", "arbitrary")),
        name="retention",
    )(inv_freq, u, u, u, u, gn_w.reshape(1, D_MIX))


def _causal_conv(x, w_ref, ext_ref, tail_ref):
    tb = x.shape[0]
    ext_ref[0:TAIL, :] = tail_ref[...]
    ext_ref[TAIL:, :] = x
    tail_ref[...] = x[tb - TAIL:, :]
    y = None
    for j in range(CONV_WIDTH):
        off = TAIL - (CONV_WIDTH - 1) + j
        term = w_ref[j:j + 1, :] * ext_ref[off:off + tb, :]
        y = term if y is None else y + term
    return y


def _rglru_kernel(x_ref, gate_ref, cw_ref, cb_ref, wr_ref, br_ref, wi_ref, bi_ref, a_ref, o_ref,
                  ext_ref, tail_ref, h_ref, *, tb):
    i = pl.program_id(0)

    @pl.when(i == 0)
    def _():
        h_ref[...] = jnp.zeros_like(h_ref)
        tail_ref[...] = jnp.zeros_like(tail_ref)

    xc = _causal_conv(x_ref[...], cw_ref, ext_ref, tail_ref) + cb_ref[...]
    xb = _bf(xc)
    r_parts, i_parts = [], []
    for n in range(LRU_BLOCKS):
        blk = xb[:, n * LRU_BLOCK:(n + 1) * LRU_BLOCK]
        r_parts.append(_dot(blk, wr_ref[n]))
        i_parts.append(_dot(blk, wi_ref[n]))
    r = jax.nn.sigmoid(jnp.concatenate(r_parts, axis=-1) + br_ref[...])
    ig = jax.nn.sigmoid(jnp.concatenate(i_parts, axis=-1) + bi_ref[...])
    log_a = -LRU_C * r * _softplus(-a_ref[...])
    a = jnp.exp(log_a)
    t = jnp.tanh(log_a)
    b = jnp.sqrt(-2.0 * t / (1.0 - t)) * (ig * xc)
    pos = _row_iota(a.shape)
    s = 1
    while s < tb:
        keep = pos >= s
        a_prev = jnp.where(keep, pltpu.roll(a, s, 0), 1.0)
        b_prev = jnp.where(keep, pltpu.roll(b, s, 0), 0.0)
        b = a * b_prev + b
        a = a * a_prev
        s *= 2
    hh = a * h_ref[...] + b
    h_ref[...] = hh[tb - 1:tb, :]
    o_ref[...] = (hh * jax.nn.gelu(gate_ref[...])).astype(o_ref.dtype)


def rglru(u, col0, conv_w, conv_b, w_r, b_r, w_i, b_i, a_param, tb):
    s = u.shape[0]
    base = col0 // D_MIX
    row = lambda t: t.reshape(1, D_MIX)
    vec = pl.BlockSpec((1, D_MIX), lambda i: (0, 0))
    wblk = pl.BlockSpec((LRU_BLOCKS, LRU_BLOCK, LRU_BLOCK), lambda i: (0, 0, 0))
    return pl.pallas_call(
        functools.partial(_rglru_kernel, tb=tb),
        grid=(s // tb,),
        in_specs=[pl.BlockSpec((tb, D_MIX), lambda i: (i, base)), pl.BlockSpec((tb, D_MIX), lambda i: (i, base + 1)),
                  pl.BlockSpec((CONV_WIDTH, D_MIX), lambda i: (0, 0)), vec, wblk, vec, wblk, vec, vec],
        out_specs=pl.BlockSpec((tb, D_MIX), lambda i: (i, 0)),
        out_shape=jax.ShapeDtypeStruct((s, D_MIX), BF16),
        scratch_shapes=[pltpu.VMEM((tb + TAIL, D_MIX), F32), pltpu.VMEM((TAIL, D_MIX), F32), pltpu.VMEM((1, D_MIX), F32)],
        compiler_params=_params(("arbitrary",)),
        name="rglru",
    )(u, u, conv_w, row(conv_b), _bf(w_r), row(b_r), _bf(w_i), row(b_i), row(a_param))


def _hgrn2_kernel(q_ref, f_ref, v_ref, g_ref, lbl_ref, nw_ref, o_ref, state_ref, *, tb, layer):
    i = pl.program_id(0)
    h = pl.program_id(1)

    @pl.when(i == 0)
    def _():
        state_ref[h] = jnp.zeros((HG_DK, HG_DK), F32)

    logits = lbl_ref[...]
    e = jnp.exp(logits - jnp.max(logits, axis=0, keepdims=True))
    sm = e / jnp.sum(e, axis=0, keepdims=True)
    lb = jnp.zeros((1, HG_DK), F32)
    for l in range(1, layer + 1):
        lb = lb + sm[l:l + 1, :]

    f = lb + (1.0 - lb) * jax.nn.sigmoid(f_ref[...])
    cum = _segment_cumsum(jnp.log(f), CHUNK)
    k = 1.0 - f
    q = _silu(q_ref[...]) * HG_DK ** -0.5
    v = v_ref[...]

    sub_pos = _row_iota((tb, 1)) % SUB
    o_diag = jnp.zeros((tb, HG_DK), F32)
    for d in range(SUB):
        valid = sub_pos >= d
        k_d = k if d == 0 else pltpu.roll(k, d, 0)
        v_d = v if d == 0 else pltpu.roll(v, d, 0)
        c_d = cum if d == 0 else pltpu.roll(cum, d, 0)
        w = jnp.sum(q * k_d * jnp.exp(jnp.where(valid, cum - c_d, -jnp.inf)), axis=-1, keepdims=True)
        o_diag = o_diag + w * v_d

    state = state_ref[h]
    n_sub = CHUNK // SUB
    blk_i = _row_iota((CHUNK, CHUNK)) // SUB
    outs = []
    for c in range(tb // CHUNK):
        rows = slice(c * CHUNK, (c + 1) * CHUNK)
        qc, kc, vc, cc = q[rows], k[rows], _bf(v[rows]), cum[rows]
        last = cc[CHUNK - 1:CHUNK, :]
        o = _dot_nt(_bf(qc * jnp.exp(cc)), _bf(state))
        row_blk = _row_iota((CHUNK, 1)) // SUB
        ref_rows = [cc[a * SUB - 1:a * SUB, :] for a in range(1, n_sub)]
        r_q = jnp.zeros((CHUNK, HG_DK), F32)
        for a in range(1, n_sub):
            r_q = jnp.where(row_blk == a, ref_rows[a - 1], r_q)
        qs = jnp.where(row_blk >= 1, qc * jnp.exp(cc - r_q), 0.0)
        k_parts = []
        for a in range(1, n_sub):
            before = row_blk < a
            k_parts.append(kc * jnp.exp(jnp.where(before, ref_rows[a - 1] - cc, -jnp.inf)))
        m = _dot_nt(_bf(qs), _bf(jnp.concatenate(k_parts, axis=0)))
        p = jnp.zeros((CHUNK, CHUNK), F32)
        for a in range(1, n_sub):
            p = jnp.where(blk_i == a, m[:, (a - 1) * CHUNK:a * CHUNK], p)
        o = o + _dot(_bf(p), vc)
        state = state * jnp.exp(last) + _dot_tn(vc, _bf(kc * jnp.exp(last - cc)))
        outs.append(o)
    state_ref[h] = state
    o = jnp.concatenate(outs, axis=0) + o_diag
    o = o * lax.rsqrt(jnp.mean(o * o, axis=-1, keepdims=True) + EPS) * nw_ref[...]
    o_ref[...] = (o * _silu(g_ref[...])).astype(o_ref.dtype)


def hgrn2(u, col0, lb_logits, norm_w, layer, tb):
    s = u.shape[0]
    depth = lb_logits.shape[0]
    base = col0 // HG_DK
    per = D_MIX // HG_DK

    def col(part):
        return pl.BlockSpec((tb, HG_DK), lambda i, h: (i, base + part * per + h))

    return pl.pallas_call(
        functools.partial(_hgrn2_kernel, tb=tb, layer=layer),
        grid=(s // tb, HG_HEADS),
        in_specs=[col(0), col(1), col(2), col(3),
                  pl.BlockSpec((depth, HG_DK), lambda i, h: (0, h)), pl.BlockSpec((1, HG_DK), lambda i, h: (0, h))],
        out_specs=pl.BlockSpec((tb, HG_DK), lambda i, h: (i, h)),
        out_shape=jax.ShapeDtypeStruct((s, D_MIX), BF16),
        scratch_shapes=[pltpu.VMEM((HG_HEADS, HG_DK, HG_DK), F32)],
        compiler_params=_params(("arbitrary", "arbitrary")),
        name="hgrn2",
    )(u, u, u, u, lb_logits, norm_w.reshape(1, D_MIX))


def _unit_lower_inverse(low):
    ci = _row_iota((CHUNK, CHUNK))
    cj = _col_iota((CHUNK, CHUNK))
    eye = (ci == cj).astype(F32)
    on_diag_block = (ci // SUB) == (cj // SUB)
    m = jnp.where(on_diag_block, -low, 0.0)
    off = jnp.where(on_diag_block, 0.0, low)
    inv_d = eye + m
    p = m
    for _ in range(int(math.log2(SUB)) - 1):
        p = _dot_f32(p, p)
        inv_d = _dot_f32(inv_d, eye + p)
    n = -_dot_f32(inv_d, off)
    inv_o = eye + n
    p = n
    for _ in range(int(math.log2(CHUNK // SUB)) - 1):
        p = _dot_f32(p, p)
        inv_o = _dot_f32(inv_o, eye + p)
    return _dot_f32(inv_o, inv_d)


def _deltanet_kernel(q_ref, k_ref, v_ref, z_ref, ab_ref, cwq_ref, cwk_ref, cwv_ref, alog_ref, dt_ref, nw_ref, o_ref,
                     ext_ref, tail_ref, state_ref, *, tb):
    i = pl.program_id(0)
    h = pl.program_id(1)

    @pl.when(i == 0)
    def _():
        state_ref[h] = jnp.zeros((DN_DK, DN_DK), F32)
        tail_ref[h] = jnp.zeros((3, TAIL, DN_DK), F32)

    def l2n(t):
        return t * lax.rsqrt(jnp.sum(t * t, axis=-1, keepdims=True) + EPS)

    q = l2n(_silu(_causal_conv(q_ref[...], cwq_ref, ext_ref, tail_ref.at[h, 0]))) * DN_DK ** -0.5
    k = l2n(_silu(_causal_conv(k_ref[...], cwk_ref, ext_ref, tail_ref.at[h, 1])))
    v = _silu(_causal_conv(v_ref[...], cwv_ref, ext_ref, tail_ref.at[h, 2]))

    ab = ab_ref[...]
    lane = _col_iota(ab.shape)
    log_alpha = -jnp.exp(alog_ref[...]) * _softplus(ab + dt_ref[...])
    g_all = _segment_cumsum(log_alpha, CHUNK)
    g = jnp.sum(jnp.where(lane == h, g_all, 0.0), axis=-1, keepdims=True)
    beta = jnp.sum(jnp.where(lane == h + DN_HEADS, jax.nn.sigmoid(ab), 0.0), axis=-1, keepdims=True)
    g_t = g_all.T[0:DN_HEADS, :]
    g_row = jnp.sum(jnp.where(_row_iota(g_t.shape) == h, g_t, 0.0), axis=0, keepdims=True)

    ci = _row_iota((CHUNK, CHUNK))
    cj = _col_iota((CHUNK, CHUNK))
    state = state_ref[h]
    outs = []
    for c in range(tb // CHUNK):
        rows = slice(c * CHUNK, (c + 1) * CHUNK)
        qc, kc, vc, gc, bc = q[rows], k[rows], v[rows], g[rows], beta[rows]
        decay = jnp.exp(jnp.where(cj <= ci, gc - g_row[:, rows], -jnp.inf))
        kb = _bf(kc)
        low = jnp.where(cj < ci, bc * _dot_nt(kb, kb) * decay, 0.0)
        rhs = jnp.concatenate([vc * bc, kc * (bc * jnp.exp(gc))], axis=-1)
        sol = _dot_f32(_unit_lower_inverse(low), rhs)
        u_val, w_key = sol[:, :DN_DK], sol[:, DN_DK:]
        qk = _dot_nt(_bf(qc), kb) * decay
        sb = _bf(state)
        v_new = u_val - _dot(_bf(w_key), sb)
        vb = _bf(v_new)
        outs.append(_dot(_bf(qc * jnp.exp(gc)), sb) + _dot(_bf(qk), vb))
        g_last = gc[CHUNK - 1:CHUNK, :]
        state = jnp.exp(g_last) * state + _dot_tn(_bf(kc * jnp.exp(g_last - gc)), vb)
    state_ref[h] = state
    o = jnp.concatenate(outs, axis=0)
    o = o * lax.rsqrt(jnp.mean(o * o, axis=-1, keepdims=True) + EPS) * nw_ref[...]
    o_ref[...] = (o * _silu(z_ref[...])).astype(o_ref.dtype)


def deltanet(u, col0, ab, conv_w, a_log, dt_bias, norm_w, tb):
    s = u.shape[0]
    base = col0 // DN_DK
    per = D_MIX // DN_DK

    def col(part):
        return pl.BlockSpec((tb, DN_DK), lambda i, h: (i, base + part * per + h))

    def cw(part):
        return pl.BlockSpec((CONV_WIDTH, DN_DK), lambda i, h: (0, part * per + h))

    def lanes(t):
        return jnp.zeros((1, LANES), F32).at[0, :DN_HEADS].set(t)

    small = pl.BlockSpec((1, LANES), lambda i, h: (0, 0))
    return pl.pallas_call(
        functools.partial(_deltanet_kernel, tb=tb),
        grid=(s // tb, DN_HEADS),
        in_specs=[col(0), col(1), col(2), col(3), pl.BlockSpec((tb, LANES), lambda i, h: (i, 0)),
                  cw(0), cw(1), cw(2), small, small, small],
        out_specs=pl.BlockSpec((tb, DN_DK), lambda i, h: (i, h)),
        out_shape=jax.ShapeDtypeStruct((s, D_MIX), BF16),
        scratch_shapes=[pltpu.VMEM((tb + TAIL, DN_DK), F32), pltpu.VMEM((DN_HEADS, 3, TAIL, DN_DK), F32),
                        pltpu.VMEM((DN_HEADS, DN_DK, DN_DK), F32)],
        compiler_params=_params(("arbitrary", "arbitrary")),
        name="deltanet",
    )(u, u, u, u, ab, conv_w, conv_w, conv_w, lanes(a_log), lanes(dt_bias), norm_w.reshape(1, DN_DK))


def kernel(x, norm_mix, w_in, merge_bias, ret_gn, lru_conv_w, lru_conv_b, lru_w_r, lru_b_r, lru_w_i, lru_b_i, lru_a,
           hg_lb_logits, hg_norm, dn_conv_w, dn_a_log, dn_dt_bias, dn_norm, w_branch, w_out, norm_ffn,
           w_ffn_gate, w_ffn_up, w_ffn_down, norm_final):
    b, s, d = x.shape
    depth = w_in.shape[0]
    n_mix = 14 * D_MIX
    n_ab = 2 * DN_HEADS
    hidden = w_ffn_gate.shape[-1]
    tb = min(256, s)
    outs = []
    for bi in range(b):
        xs = x.reshape(s, d) if b == 1 else x[bi]
        for l in range(depth):
            hn = rmsnorm(xs, norm_mix[l], BF16)
            w_mix = _bf(w_in[l, :, :n_mix])
            w_ab = _bf(jnp.pad(w_in[l, :, n_mix:n_mix + n_ab], ((0, 0), (0, LANES - n_ab))))
            w_gate = _bf(w_in[l, :, n_mix + n_ab:])
            u = matmul(hn, w_mix, F32, 1024, 1024, "proj_mix")
            ab = matmul(hn, w_ab, F32, 1024, LANES, "proj_ab")
            gl = matmul(hn, w_gate, F32, 1024, 1024, "proj_gate")
            y_ret = retention(u, 0, ret_gn[l], tb)
            y_lru = rglru(u, 4 * D_MIX, lru_conv_w[l], lru_conv_b[l], lru_w_r[l], lru_b_r[l], lru_w_i[l], lru_b_i[l],
                          lru_a[l], tb)
            y_hg = hgrn2(u, 6 * D_MIX, hg_lb_logits, hg_norm[l], l, tb)
            y_dn = deltanet(u, 10 * D_MIX, ab, dn_conv_w[l], dn_a_log[l], dn_dt_bias[l], dn_norm[l], tb)
            merged = merge(gl, merge_bias[l], (y_ret, y_lru, y_hg, y_dn), _bf(w_branch[l]), 1024, 512)
            xs = matmul_residual(merged, _bf(w_out[l]), xs, 1024, 1024, d, "out_proj")
            h2 = rmsnorm(xs, norm_ffn[l], BF16)
            act = ffn_up(h2, _bf(w_ffn_gate[l]), _bf(w_ffn_up[l]), 1024, 256)
            xs = matmul_residual(act, _bf(w_ffn_down[l]), xs, 1024, 512, hidden // 2, "ffn_down")
        outs.append(rmsnorm(xs, norm_final, F32))
    return outs[0].reshape(1, s, d) if b == 1 else jnp.stack(outs, axis=0)
```

```python
import functools
import math

import jax
import jax.numpy as jnp
from jax import lax
from jax.experimental import pallas as pl
from jax.experimental.pallas import tpu as pltpu

F32 = jnp.float32
BF16 = jnp.bfloat16

CHUNK = 64
HG_SUB = 16
DN_SUB = 16
D_MIX = 1024
RET_HEADS = 4
RET_DK = 256
ROPE_BASE = 10000.0
LRU_BLOCKS = 8
LRU_BLOCK = 128
LRU_C = 8.0
CONV_WIDTH = 4
HG_HEADS = 8
HG_DK = 128
DN_HEADS = 8
DN_DK = 128
DN_HPS = 4
DN_GROUP = 2 * CHUNK
EPS = 1e-6
LANES = 128
TAIL = 8

V7X_VMEM_LIMIT = 56 * 1024 * 1024

_RET_LOG_GAMMA = tuple(math.log(1.0 - 2.0 ** (-5.0 - h)) for h in range(RET_HEADS))

NT = (((1,), (1,)), ((), ()))
TN = (((0,), (0,)), ((), ()))


def _dot(a, b):
    return jnp.dot(a, b, preferred_element_type=F32)


def _dot_nt(a, b):
    return lax.dot_general(a, b, NT, preferred_element_type=F32)


def _dot_tn(a, b):
    return lax.dot_general(a, b, TN, preferred_element_type=F32)


def _bf(x):
    return x.astype(BF16)


def _silu(x):
    return x * jax.nn.sigmoid(x)


def _softplus(x):
    return jnp.maximum(x, 0.0) + jnp.log(1.0 + jnp.exp(-jnp.abs(x)))


def _params(sem, vmem=None):
    return pltpu.CompilerParams(dimension_semantics=sem, vmem_limit_bytes=vmem)


def _row_iota(shape):
    return lax.broadcasted_iota(jnp.int32, shape, 0)


def _col_iota(shape):
    return lax.broadcasted_iota(jnp.int32, shape, 1)


def _chunk_cumsum_mxu(x):
    n = x.shape[0]
    ri = _row_iota((n, n))
    ci = _col_iota((n, n))
    tri = (((ri // CHUNK) == (ci // CHUNK)) & (ci <= ri)).astype(BF16)
    hi = _bf(x)
    r1 = x - hi.astype(F32)
    mid = _bf(r1)
    lo = _bf(r1 - mid.astype(F32))
    return _dot(tri, hi) + (_dot(tri, mid) + _dot(tri, lo))


def _rmsnorm_kernel(x_ref, g_ref, o_ref):
    x = x_ref[...]
    y = x * lax.rsqrt(jnp.mean(x * x, axis=-1, keepdims=True) + EPS)
    o_ref[...] = (y * g_ref[...]).astype(o_ref.dtype)


def rmsnorm(x, g, out_dtype, tb=256):
    s, d = x.shape
    return pl.pallas_call(
        _rmsnorm_kernel,
        grid=(s // tb,),
        in_specs=[pl.BlockSpec((tb, d), lambda i: (i, 0)), pl.BlockSpec((1, d), lambda i: (0, 0))],
        out_specs=pl.BlockSpec((tb, d), lambda i: (i, 0)),
        out_shape=jax.ShapeDtypeStruct((s, d), out_dtype),
        compiler_params=_params(("arbitrary",)),
        name="rmsnorm",
    )(x, g.reshape(1, d))


def _mm_kernel(a_ref, b_ref, o_ref):
    o_ref[...] = _dot(a_ref[...], b_ref[...]).astype(o_ref.dtype)


def matmul(a, w, layer, n, out_dtype, tm, tn, name):
    m, k = a.shape
    tm, tn = min(tm, m), min(tn, n)
    return pl.pallas_call(
        _mm_kernel,
        grid=(m // tm, n // tn),
        in_specs=[pl.BlockSpec((tm, k), lambda i, j: (i, 0)), pl.BlockSpec((None, k, tn), lambda i, j: (layer, 0, j))],
        out_specs=pl.BlockSpec((tm, tn), lambda i, j: (i, j)),
        out_shape=jax.ShapeDtypeStruct((m, n), out_dtype),
        compiler_params=_params(("arbitrary", "arbitrary"), V7X_VMEM_LIMIT),
        name=name,
    )(a, w)


def _mm_res_kernel(a_ref, b_ref, r_ref, o_ref):
    @pl.when(pl.program_id(2) == 0)
    def _():
        o_ref[...] = r_ref[...]

    o_ref[...] += _dot(a_ref[...], b_ref[...])


def matmul_residual(a, w, layer, res, tm, tn, tk, name):
    m, k = a.shape
    n = w.shape[2]
    tm, tn = min(tm, m), min(tn, n)
    return pl.pallas_call(
        _mm_res_kernel,
        grid=(m // tm, n // tn, k // tk),
        in_specs=[pl.BlockSpec((tm, tk), lambda i, j, kk: (i, kk)),
                  pl.BlockSpec((None, tk, tn), lambda i, j, kk: (layer, kk, j)),
                  pl.BlockSpec((tm, tn), lambda i, j, kk: (i, j))],
        out_specs=pl.BlockSpec((tm, tn), lambda i, j, kk: (i, j)),
        out_shape=jax.ShapeDtypeStruct((m, n), F32),
        compiler_params=_params(("arbitrary", "arbitrary", "arbitrary"), V7X_VMEM_LIMIT),
        name=name,
    )(a, w, res)


def _ffn_up_kernel(h_ref, wg_ref, wu_ref, o_ref):
    h = h_ref[...]
    o_ref[...] = (_silu(_dot(h, wg_ref[...])) * _dot(h, wu_ref[...])).astype(o_ref.dtype)


def ffn_up(h, wg, wu, layer, tm, tn):
    m, k = h.shape
    n = wg.shape[2]
    tm = min(tm, m)
    return pl.pallas_call(
        _ffn_up_kernel,
        grid=(m // tm, n // tn),
        in_specs=[pl.BlockSpec((tm, k), lambda i, j: (i, 0)),
                  pl.BlockSpec((None, k, tn), lambda i, j: (layer, 0, j)),
                  pl.BlockSpec((None, k, tn), lambda i, j: (layer, 0, j))],
        out_specs=pl.BlockSpec((tm, tn), lambda i, j: (i, j)),
        out_shape=jax.ShapeDtypeStruct((m, n), BF16),
        compiler_params=_params(("arbitrary", "arbitrary"), V7X_VMEM_LIMIT),
        name="ffn_up",
    )(h, wg, wu)


def _merge_kernel(g0, g1, g2, g3, b_ref, y0, y1, y2, y3, w_ref, o_ref):
    acc = None
    for n, (g_ref, y_ref) in enumerate(((g0, y0), (g1, y1), (g2, y2), (g3, y3))):
        gate = jax.nn.sigmoid(g_ref[...] + b_ref[n:n + 1, :])
        term = gate * _dot(y_ref[...], w_ref[n])
        acc = term if acc is None else acc + term
    o_ref[...] = acc.astype(o_ref.dtype)


def merge(gl, bias, ys, w_branch, layer, tm, tn):
    s = gl.shape[0]
    _, nb, dmix, d = w_branch.shape
    tm = min(tm, s)
    nj = d // tn
    g_specs = [pl.BlockSpec((tm, tn), functools.partial(lambda i, j, n: (i, n * nj + j), n=n)) for n in range(nb)]
    y_specs = [pl.BlockSpec((tm, dmix), lambda i, j: (i, 0)) for _ in range(nb)]
    return pl.pallas_call(
        _merge_kernel,
        grid=(s // tm, nj),
        in_specs=g_specs + [pl.BlockSpec((nb, tn), lambda i, j: (0, j))] + y_specs
        + [pl.BlockSpec((None, nb, dmix, tn), lambda i, j: (layer, 0, 0, j))],
        out_specs=pl.BlockSpec((tm, tn), lambda i, j: (i, j)),
        out_shape=jax.ShapeDtypeStruct((s, d), BF16),
        compiler_params=_params(("arbitrary", "arbitrary"), V7X_VMEM_LIMIT),
        name="merge",
    )(gl, gl, gl, gl, bias.reshape(nb, d), *ys, w_branch)


def _retention_kernel(inv_ref, q_ref, k_ref, v_ref, g_ref, gn_ref, o_ref, state_ref, *, tb):
    i = pl.program_id(0)
    h = pl.program_id(1)

    @pl.when(i == 0)
    def _():
        state_ref[h] = jnp.zeros((RET_DK, RET_DK), F32)

    half = RET_DK // 2
    pos = (i * tb + _row_iota((tb, 1))).astype(F32)
    ang = pos * inv_ref[...]
    cos, sin = jnp.cos(ang), jnp.sin(ang)

    def rotary(t):
        t1, t2 = t[:, :half], t[:, half:]
        return jnp.concatenate([t1 * cos - t2 * sin, t1 * sin + t2 * cos], axis=-1)

    q = rotary(q_ref[...]) * RET_DK ** -0.5
    k = rotary(k_ref[...])
    v = v_ref[...]

    log_gamma = jnp.float32(_RET_LOG_GAMMA[0])
    for hh in range(1, RET_HEADS):
        log_gamma = jnp.where(h == hh, jnp.float32(_RET_LOG_GAMMA[hh]), log_gamma)
    ci = _row_iota((CHUNK, CHUNK))
    cj = _col_iota((CHUNK, CHUNK))
    intra = jnp.exp(log_gamma * jnp.abs(ci - cj).astype(F32))
    cpos = _row_iota((CHUNK, 1)).astype(F32)
    q_decay = jnp.exp(log_gamma * cpos)
    k_decay = jnp.exp(log_gamma * (CHUNK - cpos))
    chunk_decay = jnp.exp(jnp.full((1, 1), log_gamma * CHUNK, F32))

    state = state_ref[h]
    outs = []
    for c in range(tb // CHUNK):
        rows = slice(c * CHUNK, (c + 1) * CHUNK)
        qc, kc, vc = q[rows], k[rows], _bf(v[rows])
        scores = _dot_nt(_bf(qc), _bf(kc)) * intra
        o = _dot(_bf(scores), vc) + _dot(_bf(qc * q_decay), _bf(state))
        state = chunk_decay * state + _dot_tn(_bf(kc * k_decay), vc)
        outs.append(o)
    state_ref[h] = state
    o = jnp.concatenate(outs, axis=0)
    mu = jnp.mean(o, axis=-1, keepdims=True)
    var = jnp.mean(jnp.square(o - mu), axis=-1, keepdims=True)
    o = (o - mu) * lax.rsqrt(var + EPS) * gn_ref[...]
    o_ref[...] = (_silu(g_ref[...]) * o).astype(o_ref.dtype)


def retention(u, col0, gn_w, tb):
    s = u.shape[0]
    half = RET_DK // 2
    inv_freq = (1.0 / (ROPE_BASE ** (jnp.arange(half, dtype=F32) / half))).reshape(1, half)
    base = col0 // RET_DK
    per = D_MIX // RET_DK

    def col(part):
        return pl.BlockSpec((tb, RET_DK), lambda i, h: (i, base + part * per + h))

    return pl.pallas_call(
        functools.partial(_retention_kernel, tb=tb),
        grid=(s // tb, RET_HEADS),
        in_specs=[pl.BlockSpec((1, half), lambda i, h: (0, 0)), col(0), col(1), col(2), col(3),
                  pl.BlockSpec((1, RET_DK), lambda i, h: (0, h))],
        out_specs=pl.BlockSpec((tb, RET_DK), lambda i, h: (i, h)),
        out_shape=jax.ShapeDtypeStruct((s, D_MIX), BF16),
        scratch_shapes=[pltpu.VMEM((RET_HEADS, RET_DK, RET_DK), F32)],
        compiler_params=_params(("arbitrary", "arbitrary")),
        name="retention",
    )(inv_freq, u, u, u, u, gn_w.reshape(1, D_MIX))


def _causal_conv(x, w_ref, ext_ref, tail_ref):
    tb = x.shape[0]
    ext_ref[0:TAIL, :] = tail_ref[...]
    ext_ref[TAIL:, :] = x
    tail_ref[...] = x[tb - TAIL:, :]
    y = None
    for j in range(CONV_WIDTH):
        off = TAIL - (CONV_WIDTH - 1) + j
        term = w_ref[j:j + 1, :] * ext_ref[off:off + tb, :]
        y = term if y is None else y + term
    return y


def _rglru_kernel(x_ref, gate_ref, cw_ref, cb_ref, wr_ref, br_ref, wi_ref, bi_ref, a_ref, o_ref,
                  ext_ref, tail_ref, h_ref, *, tb):
    i = pl.program_id(0)

    @pl.when(i == 0)
    def _():
        h_ref[...] = jnp.zeros_like(h_ref)
        tail_ref[...] = jnp.zeros_like(tail_ref)

    xc = _causal_conv(x_ref[...], cw_ref, ext_ref, tail_ref) + cb_ref[...]
    xb = _bf(xc)
    r_parts, i_parts = [], []
    for n in range(LRU_BLOCKS):
        blk = xb[:, n * LRU_BLOCK:(n + 1) * LRU_BLOCK]
        r_parts.append(_dot(blk, wr_ref[n]))
        i_parts.append(_dot(blk, wi_ref[n]))
    r = jax.nn.sigmoid(jnp.concatenate(r_parts, axis=-1) + br_ref[...])
    ig = jax.nn.sigmoid(jnp.concatenate(i_parts, axis=-1) + bi_ref[...])
    log_a = -LRU_C * r * _softplus(-a_ref[...])
    a = jnp.exp(log_a)
    t = jnp.tanh(log_a)
    b = jnp.sqrt(-2.0 * t / (1.0 - t)) * (ig * xc)
    pos = _row_iota(a.shape)
    s = 1
    while s < tb:
        keep = pos >= s
        a_prev = jnp.where(keep, pltpu.roll(a, s, 0), 1.0)
        b_prev = jnp.where(keep, pltpu.roll(b, s, 0), 0.0)
        b = a * b_prev + b
        a = a * a_prev
        s *= 2
    hh = a * h_ref[...] + b
    h_ref[...] = hh[tb - 1:tb, :]
    o_ref[...] = (hh * jax.nn.gelu(gate_ref[...])).astype(o_ref.dtype)


def rglru(u, col0, conv_w, conv_b, w_r, b_r, w_i, b_i, a_param, tb):
    s = u.shape[0]
    base = col0 // D_MIX
    row = lambda t: t.reshape(1, D_MIX)
    vec = pl.BlockSpec((1, D_MIX), lambda i: (0, 0))
    wblk = pl.BlockSpec((LRU_BLOCKS, LRU_BLOCK, LRU_BLOCK), lambda i: (0, 0, 0))
    return pl.pallas_call(
        functools.partial(_rglru_kernel, tb=tb),
        grid=(s // tb,),
        in_specs=[pl.BlockSpec((tb, D_MIX), lambda i: (i, base)), pl.BlockSpec((tb, D_MIX), lambda i: (i, base + 1)),
                  pl.BlockSpec((CONV_WIDTH, D_MIX), lambda i: (0, 0)), vec, wblk, vec, wblk, vec, vec],
        out_specs=pl.BlockSpec((tb, D_MIX), lambda i: (i, 0)),
        out_shape=jax.ShapeDtypeStruct((s, D_MIX), BF16),
        scratch_shapes=[pltpu.VMEM((tb + TAIL, D_MIX), F32), pltpu.VMEM((TAIL, D_MIX), F32), pltpu.VMEM((1, D_MIX), F32)],
        compiler_params=_params(("arbitrary",)),
        name="rglru",
    )(u, u, conv_w, row(conv_b), _bf(w_r), row(b_r), _bf(w_i), row(b_i), row(a_param))


def _hgrn2_kernel(q_ref, f_ref, v_ref, g_ref, lbl_ref, nw_ref, o_ref, state_ref, *, tb, layer):
    i = pl.program_id(0)
    h = pl.program_id(1)

    @pl.when(i == 0)
    def _():
        state_ref[h] = jnp.zeros((HG_DK, HG_DK), F32)

    logits = lbl_ref[...]
    e = jnp.exp(logits - jnp.max(logits, axis=0, keepdims=True))
    sm = e / jnp.sum(e, axis=0, keepdims=True)
    lb = jnp.zeros((1, HG_DK), F32)
    for l in range(1, layer + 1):
        lb = lb + sm[l:l + 1, :]

    f = lb + (1.0 - lb) * jax.nn.sigmoid(f_ref[...])
    cum = _chunk_cumsum_mxu(jnp.log(f))
    k = 1.0 - f
    q = _silu(q_ref[...]) * HG_DK ** -0.5
    v = v_ref[...]

    sub_pos = _row_iota((tb, 1)) % HG_SUB
    o_diag = jnp.zeros((tb, HG_DK), F32)
    for d in range(HG_SUB):
        valid = sub_pos >= d
        k_d = k if d == 0 else pltpu.roll(k, d, 0)
        v_d = v if d == 0 else pltpu.roll(v, d, 0)
        c_d = cum if d == 0 else pltpu.roll(cum, d, 0)
        w = jnp.sum(q * k_d * jnp.exp(jnp.where(valid, cum - c_d, -jnp.inf)), axis=-1, keepdims=True)
        o_diag = o_diag + w * v_d

    state = state_ref[h]
    n_sub = CHUNK // HG_SUB
    blk_i = _row_iota((CHUNK, CHUNK)) // HG_SUB
    outs = []
    for c in range(tb // CHUNK):
        rows = slice(c * CHUNK, (c + 1) * CHUNK)
        qc, kc, vc, cc = q[rows], k[rows], _bf(v[rows]), cum[rows]
        last = cc[CHUNK - 1:CHUNK, :]
        o = _dot_nt(_bf(qc * jnp.exp(cc)), _bf(state))
        row_blk = _row_iota((CHUNK, 1)) // HG_SUB
        ref_rows = [cc[a * HG_SUB - 1:a * HG_SUB, :] for a in range(1, n_sub)]
        r_q = jnp.zeros((CHUNK, HG_DK), F32)
        for a in range(1, n_sub):
            r_q = jnp.where(row_blk == a, ref_rows[a - 1], r_q)
        qs = jnp.where(row_blk >= 1, qc * jnp.exp(cc - r_q), 0.0)
        k_parts = []
        for a in range(1, n_sub):
            before = row_blk < a
            k_parts.append(kc * jnp.exp(jnp.where(before, ref_rows[a - 1] - cc, -jnp.inf)))
        m = _dot_nt(_bf(qs), _bf(jnp.concatenate(k_parts, axis=0)))
        p = jnp.zeros((CHUNK, CHUNK), F32)
        for a in range(1, n_sub):
            p = jnp.where(blk_i == a, m[:, (a - 1) * CHUNK:a * CHUNK], p)
        o = o + _dot(_bf(p), vc)
        state = state * jnp.exp(last) + _dot_tn(vc, _bf(kc * jnp.exp(last - cc)))
        outs.append(o)
    state_ref[h] = state
    o = jnp.concatenate(outs, axis=0) + o_diag
    o = o * lax.rsqrt(jnp.mean(o * o, axis=-1, keepdims=True) + EPS) * nw_ref[...]
    o_ref[...] = (o * _silu(g_ref[...])).astype(o_ref.dtype)


def hgrn2(u, col0, lb_logits, norm_w, layer, tb):
    s = u.shape[0]
    depth = lb_logits.shape[0]
    base = col0 // HG_DK
    per = D_MIX // HG_DK

    def col(part):
        return pl.BlockSpec((tb, HG_DK), lambda i, h: (i, base + part * per + h))

    return pl.pallas_call(
        functools.partial(_hgrn2_kernel, tb=tb, layer=layer),
        grid=(s // tb, HG_HEADS),
        in_specs=[col(0), col(1), col(2), col(3),
                  pl.BlockSpec((depth, HG_DK), lambda i, h: (0, h)), pl.BlockSpec((1, HG_DK), lambda i, h: (0, h))],
        out_specs=pl.BlockSpec((tb, HG_DK), lambda i, h: (i, h)),
        out_shape=jax.ShapeDtypeStruct((s, D_MIX), BF16),
        scratch_shapes=[pltpu.VMEM((HG_HEADS, HG_DK, HG_DK), F32)],
        compiler_params=_params(("arbitrary", "arbitrary")),
        name="hgrn2",
    )(u, u, u, u, lb_logits, norm_w.reshape(1, D_MIX))


def _split(a):
    hi = _bf(a)
    return hi, _bf(a - hi.astype(F32))


def _dot_split(a, b):
    ah, al = a
    bh, bl = b
    return _dot(ah, bh) + (_dot(ah, bl) + _dot(al, bh))


def _unit_lower_solve(lows, rhss):
    ci = _row_iota(lows[0].shape)
    cj = _col_iota(lows[0].shape)
    eye = (ci == cj).astype(F32)
    on_diag_block = (ci // DN_SUB) == (cj // DN_SUB)
    neg_d = [jnp.where(on_diag_block, -low, 0.0) for low in lows]
    ps = [_split(m) for m in neg_d]
    offs = [_split(jnp.where(on_diag_block, 0.0, low)) for low in lows]
    inv_ds = [_split(eye + m) for m in neg_d]
    for _ in range(int(math.log2(DN_SUB)) - 1):
        sqs = [_dot_split(p, p) for p in ps]
        ps = [_split(sq) for sq in sqs]
        inv_ds = [_split(_dot_split(d, _split(eye + sq))) for d, sq in zip(inv_ds, sqs)]
    xs = [_dot(d[0], r) + _dot(d[1], r) for d, r in zip(inv_ds, rhss)]
    ns = [_split(-_dot_split(d, off)) for d, off in zip(inv_ds, offs)]
    for step in range(int(math.log2(CHUNK // DN_SUB))):
        xs = [x + _dot_split(n, _split(x)) for x, n in zip(xs, ns)]
        if step + 1 < int(math.log2(CHUNK // DN_SUB)):
            ns = [_split(_dot_split(n, n)) for n in ns]
    return xs


def _deltanet_kernel(q_ref, k_ref, v_ref, z_ref, ab_ref, cwq_ref, cwk_ref, cwv_ref, alog_ref, dt_ref, nw_ref, o_ref,
                     ext_ref, tail_ref, state_ref, *, tb):
    i = pl.program_id(0)
    hblk = pl.program_id(1)

    def l2n(t):
        return t * lax.rsqrt(jnp.sum(t * t, axis=-1, keepdims=True) + EPS)

    ab = ab_ref[...]
    lane = _col_iota(ab.shape)
    log_alpha = -jnp.exp(alog_ref[...]) * _softplus(ab + dt_ref[...])
    g_all = _chunk_cumsum_mxu(log_alpha)
    beta_all = jax.nn.sigmoid(ab)
    g_t = g_all.T[0:DN_HEADS, :]

    ri = _row_iota((DN_GROUP, DN_GROUP))
    ci = _col_iota((DN_GROUP, DN_GROUP))
    same_chunk = (ri // CHUNK) == (ci // CHUNK)
    incl = same_chunk & (ci <= ri)
    strict = same_chunk & (ci < ri)
    n_chunk = tb // CHUNK
    per_group = DN_GROUP // CHUNK

    h0 = hblk * DN_HPS

    @pl.when(i == 0)
    def _():
        state_ref[pl.ds(h0, DN_HPS)] = jnp.zeros((DN_HPS, DN_DK, DN_DK), F32)
        tail_ref[pl.ds(h0, DN_HPS)] = jnp.zeros((DN_HPS, 3, TAIL, DN_DK), F32)

    heads = []
    for hh in range(DN_HPS):
        h = h0 + hh
        lanes = slice(hh * DN_DK, (hh + 1) * DN_DK)

        def conv(x_ref, w_ref, part):
            return _silu(_causal_conv(x_ref[:, lanes], w_ref.at[:, lanes], ext_ref.at[hh * 3 + part], tail_ref.at[h, part]))

        q = l2n(conv(q_ref, cwq_ref, 0)) * DN_DK ** -0.5
        k = l2n(conv(k_ref, cwk_ref, 1))
        v = conv(v_ref, cwv_ref, 2)
        g = jnp.sum(jnp.where(lane == h, g_all, 0.0), axis=-1, keepdims=True)
        beta = jnp.sum(jnp.where(lane == h + DN_HEADS, beta_all, 0.0), axis=-1, keepdims=True)
        g_row = jnp.sum(jnp.where(_row_iota(g_t.shape) == h, g_t, 0.0), axis=0, keepdims=True)
        kb = _bf(k)
        qb = _bf(q)
        eg = jnp.exp(g)
        rhs = _bf(jnp.concatenate([v * beta, k * (beta * eg)], axis=-1))
        lows, rhss, qks = [], [], []
        for grp in range(tb // DN_GROUP):
            rows = slice(grp * DN_GROUP, (grp + 1) * DN_GROUP)
            decay = jnp.exp(jnp.where(incl, g[rows] - g_row[:, rows], -jnp.inf))
            lows.append(jnp.where(strict, beta[rows] * _dot_nt(kb[rows], kb[rows]) * decay, 0.0))
            rhss.append(rhs[rows])
            qks.append(_bf(_dot_nt(qb[rows], kb[rows]) * decay))
        heads.append(dict(h=h, lanes=lanes, k=k, g=g, lows=lows, rhss=rhss, qks=qks, qg=_bf(q * eg),
                          state=state_ref[h], v_parts=[], o_parts=[]))

    sols = _unit_lower_solve([m for hd in heads for m in hd["lows"]], [r for hd in heads for r in hd["rhss"]])
    for n, hd in enumerate(heads):
        sol = jnp.concatenate(sols[n * (tb // DN_GROUP):(n + 1) * (tb // DN_GROUP)], axis=0)
        hd["u"], hd["w"] = sol[:, :DN_DK], _bf(sol[:, DN_DK:])

    for c in range(n_chunk):
        rows = slice(c * CHUNK, (c + 1) * CHUNK)
        for hd in heads:
            both = _dot(jnp.concatenate([hd["w"][rows], hd["qg"][rows]], axis=0), _bf(hd["state"]))
            vb = _bf(hd["u"][rows] - both[:CHUNK])
            hd["o_parts"].append(both[CHUNK:])
            hd["v_parts"].append(vb)
            g_last = hd["g"][(c + 1) * CHUNK - 1:(c + 1) * CHUNK, :]
            k_tail = _bf(hd["k"][rows] * jnp.exp(g_last - hd["g"][rows]))
            hd["state"] = jnp.exp(g_last) * hd["state"] + _dot_tn(k_tail, vb)

    for hd in heads:
        state_ref[hd["h"]] = hd["state"]
        intra = [_dot(qk, jnp.concatenate(hd["v_parts"][n * per_group:(n + 1) * per_group], axis=0))
                 for n, qk in enumerate(hd["qks"])]
        o = jnp.concatenate(hd["o_parts"], axis=0) + jnp.concatenate(intra, axis=0)
        o = o * lax.rsqrt(jnp.mean(o * o, axis=-1, keepdims=True) + EPS) * nw_ref[...]
        o_ref[:, hd["lanes"]] = (o * _silu(z_ref[:, hd["lanes"]])).astype(o_ref.dtype)


def deltanet(u, col0, ab, conv_w, a_log, dt_bias, norm_w, tb):
    s = u.shape[0]
    wide = DN_HPS * DN_DK
    base = col0 // wide
    per = D_MIX // wide

    def col(part):
        return pl.BlockSpec((tb, wide), lambda i, h: (i, base + part * per + h))

    def cw(part):
        return pl.BlockSpec((CONV_WIDTH, wide), lambda i, h: (0, part * per + h))

    def lanes(t):
        return jnp.zeros((1, LANES), F32).at[0, :DN_HEADS].set(t)

    small = pl.BlockSpec((1, LANES), lambda i, h: (0, 0))
    return pl.pallas_call(
        functools.partial(_deltanet_kernel, tb=tb),
        grid=(s // tb, DN_HEADS // DN_HPS),
        in_specs=[col(0), col(1), col(2), col(3), pl.BlockSpec((tb, LANES), lambda i, h: (i, 0)),
                  cw(0), cw(1), cw(2), small, small, small],
        out_specs=pl.BlockSpec((tb, wide), lambda i, h: (i, h)),
        out_shape=jax.ShapeDtypeStruct((s, D_MIX), BF16),
        scratch_shapes=[pltpu.VMEM((DN_HPS * 3, tb + TAIL, DN_DK), F32), pltpu.VMEM((DN_HEADS, 3, TAIL, DN_DK), F32),
                        pltpu.VMEM((DN_HEADS, DN_DK, DN_DK), F32)],
        compiler_params=_params(("arbitrary", "arbitrary")),
        name="deltanet",
    )(u, u, u, u, ab, conv_w, conv_w, conv_w, lanes(a_log), lanes(dt_bias), norm_w.reshape(1, DN_DK))


def kernel(x, norm_mix, w_in, merge_bias, ret_gn, lru_conv_w, lru_conv_b, lru_w_r, lru_b_r, lru_w_i, lru_b_i, lru_a,
           hg_lb_logits, hg_norm, dn_conv_w, dn_a_log, dn_dt_bias, dn_norm, w_branch, w_out, norm_ffn,
           w_ffn_gate, w_ffn_up, w_ffn_down, norm_final):
    b, s, d = x.shape
    depth = w_in.shape[0]
    n_mix = 14 * D_MIX
    n_ab = 2 * DN_HEADS
    hidden = w_ffn_gate.shape[-1]
    tb = min(256, s)
    w_in_b = _bf(w_in)
    w_ab = jnp.pad(w_in_b[:, :, n_mix:n_mix + n_ab], ((0, 0), (0, 0), (0, LANES - n_ab)))
    w_gate = w_in_b[:, :, n_mix + n_ab:]
    w_branch_b, w_out_b = _bf(w_branch), _bf(w_out)
    w_fg, w_fu, w_fd = _bf(w_ffn_gate), _bf(w_ffn_up), _bf(w_ffn_down)
    outs = []
    for bi in range(b):
        xs = x.reshape(s, d) if b == 1 else x[bi]
        for l in range(depth):
            hn = rmsnorm(xs, norm_mix[l], BF16)
            u = matmul(hn, w_in_b, l, n_mix, F32, 1024, 1024, "proj_mix")
            ab = matmul(hn, w_ab, l, LANES, F32, 1024, LANES, "proj_ab")
            gl = matmul(hn, w_gate, l, w_gate.shape[2], F32, 1024, 1024, "proj_gate")
            y_ret = retention(u, 0, ret_gn[l], tb)
            y_lru = rglru(u, 4 * D_MIX, lru_conv_w[l], lru_conv_b[l], lru_w_r[l], lru_b_r[l], lru_w_i[l], lru_b_i[l],
                          lru_a[l], tb)
            y_hg = hgrn2(u, 6 * D_MIX, hg_lb_logits, hg_norm[l], l, tb)
            y_dn = deltanet(u, 10 * D_MIX, ab, dn_conv_w[l], dn_a_log[l], dn_dt_bias[l], dn_norm[l], tb)
            merged = merge(gl, merge_bias[l], (y_ret, y_lru, y_hg, y_dn), w_branch_b, l, 1024, 512)
            xs = matmul_residual(merged, w_out_b, l, xs, 1024, 1024, d, "out_proj")
            h2 = rmsnorm(xs, norm_ffn[l], BF16)
            act = ffn_up(h2, w_fg, w_fu, l, 2048, 256)
            xs = matmul_residual(act, w_fd, l, xs, 1024, 512, hidden // 2, "ffn_down")
        outs.append(rmsnorm(xs, norm_final, F32))
    return outs[0].reshape(1, s, d) if b == 1 else jnp.stack(outs, axis=0)
```

```python
import functools
import math

import jax
import jax.numpy as jnp
from jax import lax
from jax.experimental import pallas as pl
from jax.experimental.pallas import tpu as pltpu

F32 = jnp.float32
BF16 = jnp.bfloat16

CHUNK = 64
HG_SUB = 16
DN_SUB = 16
D_MIX = 1024
RET_HEADS = 4
RET_DK = 256
ROPE_BASE = 10000.0
LRU_BLOCKS = 8
LRU_BLOCK = 128
LRU_C = 8.0
CONV_WIDTH = 4
HG_HEADS = 8
HG_DK = 128
DN_HEADS = 8
DN_DK = 128
DN_HPS = 4
DN_GROUP = 2 * CHUNK
EPS = 1e-6
LANES = 128
TAIL = 8

V7X_VMEM_LIMIT = 56 * 1024 * 1024

_RET_LOG_GAMMA = tuple(math.log(1.0 - 2.0 ** (-5.0 - h)) for h in range(RET_HEADS))

NT = (((1,), (1,)), ((), ()))
TN = (((0,), (0,)), ((), ()))


def _dot(a, b):
    return jnp.dot(a, b, preferred_element_type=F32)


def _dot_nt(a, b):
    return lax.dot_general(a, b, NT, preferred_element_type=F32)


def _dot_tn(a, b):
    return lax.dot_general(a, b, TN, preferred_element_type=F32)


def _bf(x):
    return x.astype(BF16)


def _silu(x):
    return x * jax.nn.sigmoid(x)


def _softplus(x):
    return jnp.maximum(x, 0.0) + jnp.log(1.0 + jnp.exp(-jnp.abs(x)))


def _params(sem, vmem=None):
    return pltpu.CompilerParams(dimension_semantics=sem, vmem_limit_bytes=vmem)


def _row_iota(shape):
    return lax.broadcasted_iota(jnp.int32, shape, 0)


def _col_iota(shape):
    return lax.broadcasted_iota(jnp.int32, shape, 1)


def _chunk_cumsum_mxu(x):
    n = x.shape[0]
    ri = _row_iota((n, n))
    ci = _col_iota((n, n))
    tri = (((ri // CHUNK) == (ci // CHUNK)) & (ci <= ri)).astype(BF16)
    hi = _bf(x)
    r1 = x - hi.astype(F32)
    mid = _bf(r1)
    lo = _bf(r1 - mid.astype(F32))
    return _dot(tri, hi) + (_dot(tri, mid) + _dot(tri, lo))


def _rmsnorm_kernel(x_ref, g_ref, o_ref):
    x = x_ref[...]
    y = x * lax.rsqrt(jnp.mean(x * x, axis=-1, keepdims=True) + EPS)
    o_ref[...] = (y * g_ref[...]).astype(o_ref.dtype)


def rmsnorm(x, g, out_dtype, tb=512):
    s, d = x.shape
    tb = min(tb, s)
    return pl.pallas_call(
        _rmsnorm_kernel,
        grid=(s // tb,),
        in_specs=[pl.BlockSpec((tb, d), lambda i: (i, 0)), pl.BlockSpec((1, d), lambda i: (0, 0))],
        out_specs=pl.BlockSpec((tb, d), lambda i: (i, 0)),
        out_shape=jax.ShapeDtypeStruct((s, d), out_dtype),
        compiler_params=_params(("arbitrary",), V7X_VMEM_LIMIT),
        name="rmsnorm",
    )(x, g.reshape(1, d))


def _mm_kernel(a_ref, b_ref, o_ref):
    o_ref[...] = _dot(a_ref[...], _bf(b_ref[...])).astype(o_ref.dtype)


def matmul(a, w, layer, n, out_dtype, tm, tn, name):
    m, k = a.shape
    tm, tn = min(tm, m), min(tn, n)
    return pl.pallas_call(
        _mm_kernel,
        grid=(m // tm, n // tn),
        in_specs=[pl.BlockSpec((tm, k), lambda i, j: (i, 0)), pl.BlockSpec((None, k, tn), lambda i, j: (layer, 0, j))],
        out_specs=pl.BlockSpec((tm, tn), lambda i, j: (i, j)),
        out_shape=jax.ShapeDtypeStruct((m, n), out_dtype),
        compiler_params=_params(("arbitrary", "arbitrary"), V7X_VMEM_LIMIT),
        name=name,
    )(a, w)


def _mm_res_kernel(a_ref, b_ref, r_ref, o_ref):
    @pl.when(pl.program_id(2) == 0)
    def _():
        o_ref[...] = r_ref[...]

    o_ref[...] += _dot(a_ref[...], b_ref[...])


def matmul_residual(a, w, layer, res, tm, tn, tk, name):
    m, k = a.shape
    n = w.shape[2]
    tm, tn = min(tm, m), min(tn, n)
    return pl.pallas_call(
        _mm_res_kernel,
        grid=(m // tm, n // tn, k // tk),
        in_specs=[pl.BlockSpec((tm, tk), lambda i, j, kk: (i, kk)),
                  pl.BlockSpec((None, tk, tn), lambda i, j, kk: (layer, kk, j)),
                  pl.BlockSpec((tm, tn), lambda i, j, kk: (i, j))],
        out_specs=pl.BlockSpec((tm, tn), lambda i, j, kk: (i, j)),
        out_shape=jax.ShapeDtypeStruct((m, n), F32),
        compiler_params=_params(("arbitrary", "arbitrary", "arbitrary"), V7X_VMEM_LIMIT),
        name=name,
    )(a, w, res)


def _ffn_up_kernel(h_ref, wg_ref, wu_ref, o_ref):
    h = h_ref[...]
    o_ref[...] = (_silu(_dot(h, _bf(wg_ref[...]))) * _dot(h, _bf(wu_ref[...]))).astype(o_ref.dtype)


def ffn_up(h, wg, wu, layer, tm, tn):
    m, k = h.shape
    n = wg.shape[2]
    tm = min(tm, m)
    return pl.pallas_call(
        _ffn_up_kernel,
        grid=(m // tm, n // tn),
        in_specs=[pl.BlockSpec((tm, k), lambda i, j: (i, 0)),
                  pl.BlockSpec((None, k, tn), lambda i, j: (layer, 0, j)),
                  pl.BlockSpec((None, k, tn), lambda i, j: (layer, 0, j))],
        out_specs=pl.BlockSpec((tm, tn), lambda i, j: (i, j)),
        out_shape=jax.ShapeDtypeStruct((m, n), BF16),
        compiler_params=_params(("arbitrary", "arbitrary"), V7X_VMEM_LIMIT),
        name="ffn_up",
    )(h, wg, wu)


def _merge_kernel(g0, g1, g2, g3, b_ref, y0, y1, y2, y3, w_ref, o_ref):
    acc = None
    for n, (g_ref, y_ref) in enumerate(((g0, y0), (g1, y1), (g2, y2), (g3, y3))):
        gate = jax.nn.sigmoid(g_ref[...] + b_ref[n:n + 1, :])
        term = gate * _dot(y_ref[...], w_ref[n])
        acc = term if acc is None else acc + term
    o_ref[...] = acc.astype(o_ref.dtype)


def merge(gl, bias, ys, w_branch, layer, tm, tn):
    s = gl.shape[0]
    _, nb, dmix, d = w_branch.shape
    tm = min(tm, s)
    nj = d // tn
    g_specs = [pl.BlockSpec((tm, tn), functools.partial(lambda i, j, n: (i, n * nj + j), n=n)) for n in range(nb)]
    y_specs = [pl.BlockSpec((tm, dmix), lambda i, j: (i, 0)) for _ in range(nb)]
    return pl.pallas_call(
        _merge_kernel,
        grid=(s // tm, nj),
        in_specs=g_specs + [pl.BlockSpec((nb, tn), lambda i, j: (0, j))] + y_specs
        + [pl.BlockSpec((None, nb, dmix, tn), lambda i, j: (layer, 0, 0, j))],
        out_specs=pl.BlockSpec((tm, tn), lambda i, j: (i, j)),
        out_shape=jax.ShapeDtypeStruct((s, d), BF16),
        compiler_params=_params(("arbitrary", "arbitrary"), V7X_VMEM_LIMIT),
        name="merge",
    )(gl, gl, gl, gl, bias.reshape(nb, d), *ys, w_branch)


def _retention_kernel(inv_ref, q_ref, k_ref, v_ref, g_ref, gn_ref, o_ref, state_ref, *, tb):
    i = pl.program_id(0)
    h = pl.program_id(1)

    @pl.when(i == 0)
    def _():
        state_ref[h] = jnp.zeros((RET_DK, RET_DK), F32)

    half = RET_DK // 2
    pos = (i * tb + _row_iota((tb, 1))).astype(F32)
    ang = pos * inv_ref[...]
    cos, sin = jnp.cos(ang), jnp.sin(ang)

    def rotary(t):
        t1, t2 = t[:, :half], t[:, half:]
        return jnp.concatenate([t1 * cos - t2 * sin, t1 * sin + t2 * cos], axis=-1)

    q = rotary(q_ref[...]) * RET_DK ** -0.5
    k = rotary(k_ref[...])
    v = v_ref[...]

    log_gamma = jnp.float32(_RET_LOG_GAMMA[0])
    for hh in range(1, RET_HEADS):
        log_gamma = jnp.where(h == hh, jnp.float32(_RET_LOG_GAMMA[hh]), log_gamma)
    ci = _row_iota((CHUNK, CHUNK))
    cj = _col_iota((CHUNK, CHUNK))
    intra = jnp.exp(log_gamma * jnp.abs(ci - cj).astype(F32))
    cpos = _row_iota((CHUNK, 1)).astype(F32)
    q_decay = jnp.exp(log_gamma * cpos)
    k_decay = jnp.exp(log_gamma * (CHUNK - cpos))
    chunk_decay = jnp.exp(jnp.full((1, 1), log_gamma * CHUNK, F32))

    state = state_ref[h]
    outs = []
    for c in range(tb // CHUNK):
        rows = slice(c * CHUNK, (c + 1) * CHUNK)
        qc, kc, vc = q[rows], k[rows], _bf(v[rows])
        scores = _dot_nt(_bf(qc), _bf(kc)) * intra
        o = _dot(_bf(scores), vc) + _dot(_bf(qc * q_decay), _bf(state))
        state = chunk_decay * state + _dot_tn(_bf(kc * k_decay), vc)
        outs.append(o)
    state_ref[h] = state
    o = jnp.concatenate(outs, axis=0)
    mu = jnp.mean(o, axis=-1, keepdims=True)
    var = jnp.mean(jnp.square(o - mu), axis=-1, keepdims=True)
    o = (o - mu) * lax.rsqrt(var + EPS) * gn_ref[...]
    o_ref[...] = (_silu(g_ref[...]) * o).astype(o_ref.dtype)


def retention(u, col0, gn_w, tb):
    s = u.shape[0]
    half = RET_DK // 2
    inv_freq = (1.0 / (ROPE_BASE ** (jnp.arange(half, dtype=F32) / half))).reshape(1, half)
    base = col0 // RET_DK
    per = D_MIX // RET_DK

    def col(part):
        return pl.BlockSpec((tb, RET_DK), lambda i, h: (i, base + part * per + h))

    return pl.pallas_call(
        functools.partial(_retention_kernel, tb=tb),
        grid=(s // tb, RET_HEADS),
        in_specs=[pl.BlockSpec((1, half), lambda i, h: (0, 0)), col(0), col(1), col(2), col(3),
                  pl.BlockSpec((1, RET_DK), lambda i, h: (0, h))],
        out_specs=pl.BlockSpec((tb, RET_DK), lambda i, h: (i, h)),
        out_shape=jax.ShapeDtypeStruct((s, D_MIX), BF16),
        scratch_shapes=[pltpu.VMEM((RET_HEADS, RET_DK, RET_DK), F32)],
        compiler_params=_params(("arbitrary", "arbitrary")),
        name="retention",
    )(inv_freq, u, u, u, u, gn_w.reshape(1, D_MIX))


def _causal_conv(x, w_ref, ext_ref, tail_ref):
    tb = x.shape[0]
    ext_ref[0:TAIL, :] = tail_ref[...]
    ext_ref[TAIL:, :] = x
    tail_ref[...] = x[tb - TAIL:, :]
    y = None
    for j in range(CONV_WIDTH):
        off = TAIL - (CONV_WIDTH - 1) + j
        term = w_ref[j:j + 1, :] * ext_ref[off:off + tb, :]
        y = term if y is None else y + term
    return y


def _rglru_kernel(x_ref, gate_ref, cw_ref, cb_ref, wr_ref, br_ref, wi_ref, bi_ref, a_ref, o_ref,
                  ext_ref, tail_ref, h_ref, *, tb):
    i = pl.program_id(0)

    @pl.when(i == 0)
    def _():
        h_ref[...] = jnp.zeros_like(h_ref)
        tail_ref[...] = jnp.zeros_like(tail_ref)

    xc = _causal_conv(x_ref[...], cw_ref, ext_ref, tail_ref) + cb_ref[...]
    xb = _bf(xc)
    r_parts, i_parts = [], []
    for n in range(LRU_BLOCKS):
        blk = xb[:, n * LRU_BLOCK:(n + 1) * LRU_BLOCK]
        r_parts.append(_dot(blk, wr_ref[n]))
        i_parts.append(_dot(blk, wi_ref[n]))
    r = jax.nn.sigmoid(jnp.concatenate(r_parts, axis=-1) + br_ref[...])
    ig = jax.nn.sigmoid(jnp.concatenate(i_parts, axis=-1) + bi_ref[...])
    log_a = -LRU_C * r * _softplus(-a_ref[...])
    a = jnp.exp(log_a)
    t = jnp.tanh(log_a)
    b = jnp.sqrt(-2.0 * t / (1.0 - t)) * (ig * xc)
    pos = _row_iota(a.shape)
    s = 1
    while s < tb:
        keep = pos >= s
        a_prev = jnp.where(keep, pltpu.roll(a, s, 0), 1.0)
        b_prev = jnp.where(keep, pltpu.roll(b, s, 0), 0.0)
        b = a * b_prev + b
        a = a * a_prev
        s *= 2
    hh = a * h_ref[...] + b
    h_ref[...] = hh[tb - 1:tb, :]
    o_ref[...] = (hh * jax.nn.gelu(gate_ref[...])).astype(o_ref.dtype)


def rglru(u, col0, conv_w, conv_b, w_r, b_r, w_i, b_i, a_param, tb):
    s = u.shape[0]
    base = col0 // D_MIX
    row = lambda t: t.reshape(1, D_MIX)
    vec = pl.BlockSpec((1, D_MIX), lambda i: (0, 0))
    wblk = pl.BlockSpec((LRU_BLOCKS, LRU_BLOCK, LRU_BLOCK), lambda i: (0, 0, 0))
    return pl.pallas_call(
        functools.partial(_rglru_kernel, tb=tb),
        grid=(s // tb,),
        in_specs=[pl.BlockSpec((tb, D_MIX), lambda i: (i, base)), pl.BlockSpec((tb, D_MIX), lambda i: (i, base + 1)),
                  pl.BlockSpec((CONV_WIDTH, D_MIX), lambda i: (0, 0)), vec, wblk, vec, wblk, vec, vec],
        out_specs=pl.BlockSpec((tb, D_MIX), lambda i: (i, 0)),
        out_shape=jax.ShapeDtypeStruct((s, D_MIX), BF16),
        scratch_shapes=[pltpu.VMEM((tb + TAIL, D_MIX), F32), pltpu.VMEM((TAIL, D_MIX), F32), pltpu.VMEM((1, D_MIX), F32)],
        compiler_params=_params(("arbitrary",)),
        name="rglru",
    )(u, u, conv_w, row(conv_b), _bf(w_r), row(b_r), _bf(w_i), row(b_i), row(a_param))


def _hgrn2_kernel(q_ref, f_ref, v_ref, g_ref, lbl_ref, nw_ref, o_ref, state_ref, shift_ref, *, tb, layer):
    i = pl.program_id(0)
    h = pl.program_id(1)

    @pl.when(i == 0)
    def _():
        state_ref[h] = jnp.zeros((HG_DK, HG_DK), F32)

    logits = lbl_ref[...]
    e = jnp.exp(logits - jnp.max(logits, axis=0, keepdims=True))
    sm = e / jnp.sum(e, axis=0, keepdims=True)
    lb = jnp.zeros((1, HG_DK), F32)
    for l in range(1, layer + 1):
        lb = lb + sm[l:l + 1, :]

    f = lb + (1.0 - lb) * jax.nn.sigmoid(f_ref[...])
    cum = _chunk_cumsum_mxu(jnp.log(f))
    k = 1.0 - f
    q = _silu(q_ref[...]) * HG_DK ** -0.5
    v = v_ref[...]

    @pl.when((i == 0) & (h == 0))
    def _():
        shift_ref[:, 0:HG_SUB, :] = jnp.zeros((3, HG_SUB, HG_DK), F32)

    shift_ref[0, HG_SUB:, :] = k
    shift_ref[1, HG_SUB:, :] = v
    shift_ref[2, HG_SUB:, :] = cum
    sub_pos = _row_iota((tb, 1)) % HG_SUB
    o_diag = jnp.zeros((tb, HG_DK), F32)
    for d in range(HG_SUB):
        valid = sub_pos >= d
        k_d = k if d == 0 else shift_ref[0, HG_SUB - d:HG_SUB - d + tb, :]
        v_d = v if d == 0 else shift_ref[1, HG_SUB - d:HG_SUB - d + tb, :]
        c_d = cum if d == 0 else shift_ref[2, HG_SUB - d:HG_SUB - d + tb, :]
        w = jnp.sum(q * k_d * jnp.exp(jnp.where(valid, cum - c_d, -jnp.inf)), axis=-1, keepdims=True)
        o_diag = o_diag + w * v_d

    state = state_ref[h]
    n_sub = CHUNK // HG_SUB
    row_blk = _row_iota((CHUNK, 1)) // HG_SUB
    outs = []
    for c in range(tb // CHUNK):
        rows = slice(c * CHUNK, (c + 1) * CHUNK)
        qc, kc, vc, cc = q[rows], k[rows], _bf(v[rows]), cum[rows]
        last = cc[CHUNK - 1:CHUNK, :]
        o = _dot_nt(_bf(qc * jnp.exp(cc)), _bf(state))
        p_rows = [jnp.zeros((HG_SUB, CHUNK), F32)]
        for a in range(1, n_sub):
            sub = slice(a * HG_SUB, (a + 1) * HG_SUB)
            r_a = cc[a * HG_SUB - 1:a * HG_SUB, :]
            q_a = qc[sub] * jnp.exp(cc[sub] - r_a)
            k_a = kc * jnp.exp(jnp.where(row_blk < a, r_a - cc, -jnp.inf))
            p_rows.append(_dot_nt(_bf(q_a), _bf(k_a)))
        o = o + _dot(_bf(jnp.concatenate(p_rows, axis=0)), vc)
        state = state * jnp.exp(last) + _dot_tn(vc, _bf(kc * jnp.exp(last - cc)))
        outs.append(o)
    state_ref[h] = state
    o = jnp.concatenate(outs, axis=0) + o_diag
    o = o * lax.rsqrt(jnp.mean(o * o, axis=-1, keepdims=True) + EPS) * nw_ref[...]
    o_ref[...] = (o * _silu(g_ref[...])).astype(o_ref.dtype)


def hgrn2(u, col0, lb_logits, norm_w, layer, tb):
    s = u.shape[0]
    depth = lb_logits.shape[0]
    base = col0 // HG_DK
    per = D_MIX // HG_DK

    def col(part):
        return pl.BlockSpec((tb, HG_DK), lambda i, h: (i, base + part * per + h))

    return pl.pallas_call(
        functools.partial(_hgrn2_kernel, tb=tb, layer=layer),
        grid=(s // tb, HG_HEADS),
        in_specs=[col(0), col(1), col(2), col(3),
                  pl.BlockSpec((depth, HG_DK), lambda i, h: (0, h)), pl.BlockSpec((1, HG_DK), lambda i, h: (0, h))],
        out_specs=pl.BlockSpec((tb, HG_DK), lambda i, h: (i, h)),
        out_shape=jax.ShapeDtypeStruct((s, D_MIX), BF16),
        scratch_shapes=[pltpu.VMEM((HG_HEADS, HG_DK, HG_DK), F32), pltpu.VMEM((3, tb + HG_SUB, HG_DK), F32)],
        compiler_params=_params(("arbitrary", "arbitrary")),
        name="hgrn2",
    )(u, u, u, u, lb_logits, norm_w.reshape(1, D_MIX))


def _split(a):
    hi = _bf(a)
    return hi, _bf(a - hi.astype(F32))


def _dot_split(a, b):
    ah, al = a
    bh, bl = b
    return _dot(ah, bh) + (_dot(ah, bl) + _dot(al, bh))


def _unit_lower_solve(lows, rhss):
    ci = _row_iota(lows[0].shape)
    cj = _col_iota(lows[0].shape)
    eye = (ci == cj).astype(F32)
    on_diag_block = (ci // DN_SUB) == (cj // DN_SUB)
    neg_d = [jnp.where(on_diag_block, -low, 0.0) for low in lows]
    ps = [_split(m) for m in neg_d]
    offs = [_split(jnp.where(on_diag_block, 0.0, low)) for low in lows]
    inv_ds = [_split(eye + m) for m in neg_d]
    for _ in range(int(math.log2(DN_SUB)) - 1):
        sqs = [_dot_split(p, p) for p in ps]
        ps = [_split(sq) for sq in sqs]
        inv_ds = [_split(_dot_split(d, _split(eye + sq))) for d, sq in zip(inv_ds, sqs)]
    xs = [_dot(d[0], r) + _dot(d[1], r) for d, r in zip(inv_ds, rhss)]
    ns = [_split(-_dot_split(d, off)) for d, off in zip(inv_ds, offs)]
    for step in range(int(math.log2(CHUNK // DN_SUB))):
        xs = [x + _dot_split(n, _split(x)) for x, n in zip(xs, ns)]
        if step + 1 < int(math.log2(CHUNK // DN_SUB)):
            ns = [_split(_dot_split(n, n)) for n in ns]
    return xs


def _deltanet_kernel(q_ref, k_ref, v_ref, z_ref, ab_ref, cwq_ref, cwk_ref, cwv_ref, alog_ref, dt_ref, nw_ref, o_ref,
                     ext_ref, tail_ref, state_ref, *, tb):
    i = pl.program_id(0)
    hblk = pl.program_id(1)

    def l2n(t):
        return t * lax.rsqrt(jnp.sum(t * t, axis=-1, keepdims=True) + EPS)

    ab = ab_ref[...]
    lane = _col_iota(ab.shape)
    log_alpha = -jnp.exp(alog_ref[...]) * _softplus(ab + dt_ref[...])
    g_all = _chunk_cumsum_mxu(log_alpha)
    beta_all = jax.nn.sigmoid(ab)
    g_t = g_all.T[0:DN_HEADS, :]

    ri = _row_iota((DN_GROUP, DN_GROUP))
    ci = _col_iota((DN_GROUP, DN_GROUP))
    same_chunk = (ri // CHUNK) == (ci // CHUNK)
    incl = same_chunk & (ci <= ri)
    strict = same_chunk & (ci < ri)
    n_chunk = tb // CHUNK
    per_group = DN_GROUP // CHUNK

    h0 = hblk * DN_HPS

    @pl.when(i == 0)
    def _():
        state_ref[pl.ds(h0, DN_HPS)] = jnp.zeros((DN_HPS, DN_DK, DN_DK), F32)
        tail_ref[pl.ds(h0, DN_HPS)] = jnp.zeros((DN_HPS, 3, TAIL, DN_DK), F32)

    heads = []
    for hh in range(DN_HPS):
        h = h0 + hh
        lanes = slice(hh * DN_DK, (hh + 1) * DN_DK)

        def conv(x_ref, w_ref, part):
            return _silu(_causal_conv(x_ref[:, lanes], w_ref.at[:, lanes], ext_ref.at[hh * 3 + part], tail_ref.at[h, part]))

        q = l2n(conv(q_ref, cwq_ref, 0)) * DN_DK ** -0.5
        k = l2n(conv(k_ref, cwk_ref, 1))
        v = conv(v_ref, cwv_ref, 2)
        g = jnp.sum(jnp.where(lane == h, g_all, 0.0), axis=-1, keepdims=True)
        beta = jnp.sum(jnp.where(lane == h + DN_HEADS, beta_all, 0.0), axis=-1, keepdims=True)
        g_row = jnp.sum(jnp.where(_row_iota(g_t.shape) == h, g_t, 0.0), axis=0, keepdims=True)
        kb = _bf(k)
        qb = _bf(q)
        eg = jnp.exp(g)
        rhs = _bf(jnp.concatenate([v * beta, k * (beta * eg)], axis=-1))
        lows, rhss, qks = [], [], []
        for grp in range(tb // DN_GROUP):
            rows = slice(grp * DN_GROUP, (grp + 1) * DN_GROUP)
            decay = jnp.exp(jnp.where(incl, g[rows] - g_row[:, rows], -jnp.inf))
            lows.append(jnp.where(strict, beta[rows] * _dot_nt(kb[rows], kb[rows]) * decay, 0.0))
            rhss.append(rhs[rows])
            qks.append(_bf(_dot_nt(qb[rows], kb[rows]) * decay))
        heads.append(dict(h=h, lanes=lanes, k=k, g=g, lows=lows, rhss=rhss, qks=qks, qg=_bf(q * eg),
                          state=state_ref[h], v_parts=[], o_parts=[]))

    sols = _unit_lower_solve([m for hd in heads for m in hd["lows"]], [r for hd in heads for r in hd["rhss"]])
    for n, hd in enumerate(heads):
        sol = jnp.concatenate(sols[n * (tb // DN_GROUP):(n + 1) * (tb // DN_GROUP)], axis=0)
        hd["u"], hd["w"] = sol[:, :DN_DK], _bf(sol[:, DN_DK:])

    for c in range(n_chunk):
        rows = slice(c * CHUNK, (c + 1) * CHUNK)
        for hd in heads:
            both = _dot(jnp.concatenate([hd["w"][rows], hd["qg"][rows]], axis=0), _bf(hd["state"]))
            vb = _bf(hd["u"][rows] - both[:CHUNK])
            hd["o_parts"].append(both[CHUNK:])
            hd["v_parts"].append(vb)
            g_last = hd["g"][(c + 1) * CHUNK - 1:(c + 1) * CHUNK, :]
            k_tail = _bf(hd["k"][rows] * jnp.exp(g_last - hd["g"][rows]))
            hd["state"] = jnp.exp(g_last) * hd["state"] + _dot_tn(k_tail, vb)

    for hd in heads:
        state_ref[hd["h"]] = hd["state"]
        intra = [_dot(qk, jnp.concatenate(hd["v_parts"][n * per_group:(n + 1) * per_group], axis=0))
                 for n, qk in enumerate(hd["qks"])]
        o = jnp.concatenate(hd["o_parts"], axis=0) + jnp.concatenate(intra, axis=0)
        o = o * lax.rsqrt(jnp.mean(o * o, axis=-1, keepdims=True) + EPS) * nw_ref[...]
        o_ref[:, hd["lanes"]] = (o * _silu(z_ref[:, hd["lanes"]])).astype(o_ref.dtype)


def deltanet(u, col0, ab, conv_w, a_log, dt_bias, norm_w, tb):
    s = u.shape[0]
    wide = DN_HPS * DN_DK
    base = col0 // wide
    per = D_MIX // wide

    def col(part):
        return pl.BlockSpec((tb, wide), lambda i, h: (i, base + part * per + h))

    def cw(part):
        return pl.BlockSpec((CONV_WIDTH, wide), lambda i, h: (0, part * per + h))

    def lanes(t):
        return jnp.zeros((1, LANES), F32).at[0, :DN_HEADS].set(t)

    small = pl.BlockSpec((1, LANES), lambda i, h: (0, 0))
    return pl.pallas_call(
        functools.partial(_deltanet_kernel, tb=tb),
        grid=(s // tb, DN_HEADS // DN_HPS),
        in_specs=[col(0), col(1), col(2), col(3), pl.BlockSpec((tb, LANES), lambda i, h: (i, 0)),
                  cw(0), cw(1), cw(2), small, small, small],
        out_specs=pl.BlockSpec((tb, wide), lambda i, h: (i, h)),
        out_shape=jax.ShapeDtypeStruct((s, D_MIX), BF16),
        scratch_shapes=[pltpu.VMEM((DN_HPS * 3, tb + TAIL, DN_DK), F32), pltpu.VMEM((DN_HEADS, 3, TAIL, DN_DK), F32),
                        pltpu.VMEM((DN_HEADS, DN_DK, DN_DK), F32)],
        compiler_params=_params(("arbitrary", "arbitrary")),
        name="deltanet",
    )(u, u, u, u, ab, conv_w, conv_w, conv_w, lanes(a_log), lanes(dt_bias), norm_w.reshape(1, DN_DK))


def kernel(x, norm_mix, w_in, merge_bias, ret_gn, lru_conv_w, lru_conv_b, lru_w_r, lru_b_r, lru_w_i, lru_b_i, lru_a,
           hg_lb_logits, hg_norm, dn_conv_w, dn_a_log, dn_dt_bias, dn_norm, w_branch, w_out, norm_ffn,
           w_ffn_gate, w_ffn_up, w_ffn_down, norm_final):
    b, s, d = x.shape
    depth = w_in.shape[0]
    n_mix = 14 * D_MIX
    n_ab = 2 * DN_HEADS
    hidden = w_ffn_gate.shape[-1]
    tb = min(256, s)
    w_ab = _bf(jnp.pad(w_in[:, :, n_mix:n_mix + n_ab], ((0, 0), (0, 0), (0, LANES - n_ab))))
    w_gate = _bf(w_in[:, :, n_mix + n_ab:])
    w_branch_b, w_out_b, w_fd = _bf(w_branch), _bf(w_out), _bf(w_ffn_down)
    outs = []
    for bi in range(b):
        xs = x.reshape(s, d) if b == 1 else x[bi]
        for l in range(depth):
            hn = rmsnorm(xs, norm_mix[l], BF16)
            u = matmul(hn, w_in, l, n_mix, F32, 2048, 256, "proj_mix")
            ab = matmul(hn, w_ab, l, LANES, F32, 1024, LANES, "proj_ab")
            gl = matmul(hn, w_gate, l, w_gate.shape[2], F32, 1024, 1024, "proj_gate")
            y_ret = retention(u, 0, ret_gn[l], tb)
            y_lru = rglru(u, 4 * D_MIX, lru_conv_w[l], lru_conv_b[l], lru_w_r[l], lru_b_r[l], lru_w_i[l], lru_b_i[l],
                          lru_a[l], tb)
            y_hg = hgrn2(u, 6 * D_MIX, hg_lb_logits, hg_norm[l], l, tb)
            y_dn = deltanet(u, 10 * D_MIX, ab, dn_conv_w[l], dn_a_log[l], dn_dt_bias[l], dn_norm[l], tb)
            merged = merge(gl, merge_bias[l], (y_ret, y_lru, y_hg, y_dn), w_branch_b, l, 1024, 512)
            xs = matmul_residual(merged, w_out_b, l, xs, 1024, 1024, d, "out_proj")
            h2 = rmsnorm(xs, norm_ffn[l], BF16)
            act = ffn_up(h2, w_ffn_gate, w_ffn_up, l, 2048, 256)
            xs = matmul_residual(act, w_fd, l, xs, 1024, 512, hidden // 2, "ffn_down")
        outs.append(rmsnorm(xs, norm_final, F32))
    return outs[0].reshape(1, s, d) if b == 1 else jnp.stack(outs, axis=0)
```

```python
import functools
import math

import jax
import jax.numpy as jnp
from jax import lax
from jax.experimental import pallas as pl
from jax.experimental.pallas import tpu as pltpu

F32 = jnp.float32
BF16 = jnp.bfloat16

CHUNK = 64
HG_SUB = 16
DN_SUB = 16
D_MIX = 1024
RET_HEADS = 4
RET_DK = 256
ROPE_BASE = 10000.0
LRU_BLOCKS = 8
LRU_BLOCK = 128
LRU_C = 8.0
LRU_GROUP = 8
CONV_WIDTH = 4
HG_HEADS = 8
HG_DK = 128
DN_HEADS = 8
DN_DK = 128
DN_HPS = 4
DN_GROUP = 2 * CHUNK
EPS = 1e-6
LANES = 128
TAIL = 8

V7X_VMEM_LIMIT = 56 * 1024 * 1024

_RET_LOG_GAMMA = tuple(math.log(1.0 - 2.0 ** (-5.0 - h)) for h in range(RET_HEADS))

NT = (((1,), (1,)), ((), ()))
TN = (((0,), (0,)), ((), ()))


def _dot(a, b):
    return jnp.dot(a, b, preferred_element_type=F32)


def _dot_nt(a, b):
    return lax.dot_general(a, b, NT, preferred_element_type=F32)


def _dot_tn(a, b):
    return lax.dot_general(a, b, TN, preferred_element_type=F32)


def _bf(x):
    return x.astype(BF16)


def _silu(x):
    return x * jax.nn.sigmoid(x)


def _softplus(x):
    return jnp.maximum(x, 0.0) + jnp.log(1.0 + jnp.exp(-jnp.abs(x)))


def _params(sem, vmem=None):
    return pltpu.CompilerParams(dimension_semantics=sem, vmem_limit_bytes=vmem)


def _row_iota(shape):
    return lax.broadcasted_iota(jnp.int32, shape, 0)


def _col_iota(shape):
    return lax.broadcasted_iota(jnp.int32, shape, 1)


def _chunk_cumsum_mxu(x):
    n = x.shape[0]
    ri = _row_iota((n, n))
    ci = _col_iota((n, n))
    tri = (((ri // CHUNK) == (ci // CHUNK)) & (ci <= ri)).astype(BF16)
    hi = _bf(x)
    r1 = x - hi.astype(F32)
    mid = _bf(r1)
    lo = _bf(r1 - mid.astype(F32))
    return _dot(tri, hi) + (_dot(tri, mid) + _dot(tri, lo))


def _rmsnorm_kernel(x_ref, g_ref, o_ref):
    x = x_ref[...]
    y = x * lax.rsqrt(jnp.mean(x * x, axis=-1, keepdims=True) + EPS)
    o_ref[...] = (y * g_ref[...]).astype(o_ref.dtype)


def rmsnorm(x, g, out_dtype, tb=512):
    s, d = x.shape
    tb = min(tb, s)
    return pl.pallas_call(
        _rmsnorm_kernel,
        grid=(s // tb,),
        in_specs=[pl.BlockSpec((tb, d), lambda i: (i, 0)), pl.BlockSpec((1, d), lambda i: (0, 0))],
        out_specs=pl.BlockSpec((tb, d), lambda i: (i, 0)),
        out_shape=jax.ShapeDtypeStruct((s, d), out_dtype),
        compiler_params=_params(("arbitrary",), V7X_VMEM_LIMIT),
        name="rmsnorm",
    )(x, g.reshape(1, d))


def _mm_kernel(a_ref, b_ref, o_ref):
    o_ref[...] = _dot(a_ref[...], _bf(b_ref[...])).astype(o_ref.dtype)


def matmul(a, w, layer, n, out_dtype, tm, tn, name):
    m, k = a.shape
    tm, tn = min(tm, m), min(tn, n)
    return pl.pallas_call(
        _mm_kernel,
        grid=(m // tm, n // tn),
        in_specs=[pl.BlockSpec((tm, k), lambda i, j: (i, 0)), pl.BlockSpec((None, k, tn), lambda i, j: (layer, 0, j))],
        out_specs=pl.BlockSpec((tm, tn), lambda i, j: (i, j)),
        out_shape=jax.ShapeDtypeStruct((m, n), out_dtype),
        compiler_params=_params(("arbitrary", "arbitrary"), V7X_VMEM_LIMIT),
        name=name,
    )(a, w)


def _mm_res_kernel(a_ref, b_ref, r_ref, o_ref):
    @pl.when(pl.program_id(2) == 0)
    def _():
        o_ref[...] = r_ref[...]

    o_ref[...] += _dot(a_ref[...], b_ref[...])


def matmul_residual(a, w, layer, res, tm, tn, tk, name):
    m, k = a.shape
    n = w.shape[2]
    tm, tn = min(tm, m), min(tn, n)
    return pl.pallas_call(
        _mm_res_kernel,
        grid=(m // tm, n // tn, k // tk),
        in_specs=[pl.BlockSpec((tm, tk), lambda i, j, kk: (i, kk)),
                  pl.BlockSpec((None, tk, tn), lambda i, j, kk: (layer, kk, j)),
                  pl.BlockSpec((tm, tn), lambda i, j, kk: (i, j))],
        out_specs=pl.BlockSpec((tm, tn), lambda i, j, kk: (i, j)),
        out_shape=jax.ShapeDtypeStruct((m, n), F32),
        compiler_params=_params(("arbitrary", "arbitrary", "arbitrary"), V7X_VMEM_LIMIT),
        name=name,
    )(a, w, res)


def _ffn_up_kernel(h_ref, wg_ref, wu_ref, o_ref):
    h = h_ref[...]
    o_ref[...] = (_silu(_dot(h, _bf(wg_ref[...]))) * _dot(h, _bf(wu_ref[...]))).astype(o_ref.dtype)


def ffn_up(h, wg, wu, layer, tm, tn):
    m, k = h.shape
    n = wg.shape[2]
    tm = min(tm, m)
    return pl.pallas_call(
        _ffn_up_kernel,
        grid=(m // tm, n // tn),
        in_specs=[pl.BlockSpec((tm, k), lambda i, j: (i, 0)),
                  pl.BlockSpec((None, k, tn), lambda i, j: (layer, 0, j)),
                  pl.BlockSpec((None, k, tn), lambda i, j: (layer, 0, j))],
        out_specs=pl.BlockSpec((tm, tn), lambda i, j: (i, j)),
        out_shape=jax.ShapeDtypeStruct((m, n), BF16),
        compiler_params=_params(("arbitrary", "arbitrary"), V7X_VMEM_LIMIT),
        name="ffn_up",
    )(h, wg, wu)


def _merge_kernel(g0, g1, g2, g3, b_ref, y0, y1, y2, y3, w_ref, o_ref):
    acc = None
    for n, (g_ref, y_ref) in enumerate(((g0, y0), (g1, y1), (g2, y2), (g3, y3))):
        gate = jax.nn.sigmoid(g_ref[...] + b_ref[n:n + 1, :])
        term = gate * _dot(y_ref[...], w_ref[n])
        acc = term if acc is None else acc + term
    o_ref[...] = acc.astype(o_ref.dtype)


def merge(gl, bias, ys, w_branch, layer, tm, tn):
    s = gl.shape[0]
    _, nb, dmix, d = w_branch.shape
    tm = min(tm, s)
    nj = d // tn
    g_specs = [pl.BlockSpec((tm, tn), functools.partial(lambda i, j, n: (i, n * nj + j), n=n)) for n in range(nb)]
    y_specs = [pl.BlockSpec((tm, dmix), lambda i, j: (i, 0)) for _ in range(nb)]
    return pl.pallas_call(
        _merge_kernel,
        grid=(s // tm, nj),
        in_specs=g_specs + [pl.BlockSpec((nb, tn), lambda i, j: (0, j))] + y_specs
        + [pl.BlockSpec((None, nb, dmix, tn), lambda i, j: (layer, 0, 0, j))],
        out_specs=pl.BlockSpec((tm, tn), lambda i, j: (i, j)),
        out_shape=jax.ShapeDtypeStruct((s, d), BF16),
        compiler_params=_params(("arbitrary", "arbitrary"), V7X_VMEM_LIMIT),
        name="merge",
    )(gl, gl, gl, gl, bias.reshape(nb, d), *ys, w_branch)


def _retention_kernel(inv_ref, q_ref, k_ref, v_ref, g_ref, gn_ref, o_ref, state_ref, trig_ref, *, tb):
    i = pl.program_id(0)
    h = pl.program_id(1)

    @pl.when(i == 0)
    def _():
        state_ref[h] = jnp.zeros((RET_DK, RET_DK), F32)

    half = RET_DK // 2

    @pl.when(h == 0)
    def _():
        pos = (i * tb + _row_iota((tb, 1))).astype(F32)
        ang = pos * inv_ref[...]
        trig_ref[0] = jnp.cos(ang)
        trig_ref[1] = jnp.sin(ang)

    cos, sin = trig_ref[0], trig_ref[1]

    def rotary(t):
        t1, t2 = t[:, :half], t[:, half:]
        return jnp.concatenate([t1 * cos - t2 * sin, t1 * sin + t2 * cos], axis=-1)

    q = rotary(q_ref[...]) * RET_DK ** -0.5
    k = rotary(k_ref[...])
    v = v_ref[...]

    log_gamma = jnp.float32(_RET_LOG_GAMMA[0])
    for hh in range(1, RET_HEADS):
        log_gamma = jnp.where(h == hh, jnp.float32(_RET_LOG_GAMMA[hh]), log_gamma)
    ci = _row_iota((CHUNK, CHUNK))
    cj = _col_iota((CHUNK, CHUNK))
    intra = jnp.exp(log_gamma * jnp.abs(ci - cj).astype(F32))
    cpos = _row_iota((CHUNK, 1)).astype(F32)
    q_decay = jnp.exp(log_gamma * cpos)
    k_decay = jnp.exp(log_gamma * (CHUNK - cpos))
    chunk_decay = jnp.exp(jnp.full((1, 1), log_gamma * CHUNK, F32))

    state = state_ref[h]
    outs = []
    for c in range(tb // CHUNK):
        rows = slice(c * CHUNK, (c + 1) * CHUNK)
        qc, kc, vc = q[rows], k[rows], _bf(v[rows])
        scores = _dot_nt(_bf(qc), _bf(kc)) * intra
        o = _dot(_bf(scores), vc) + _dot(_bf(qc * q_decay), _bf(state))
        state = chunk_decay * state + _dot_tn(_bf(kc * k_decay), vc)
        outs.append(o)
    state_ref[h] = state
    o = jnp.concatenate(outs, axis=0)
    mu = jnp.mean(o, axis=-1, keepdims=True)
    var = jnp.mean(jnp.square(o - mu), axis=-1, keepdims=True)
    o = (o - mu) * lax.rsqrt(var + EPS) * gn_ref[...]
    o_ref[...] = (_silu(g_ref[...]) * o).astype(o_ref.dtype)


def retention(u, col0, gn_w, tb):
    s = u.shape[0]
    half = RET_DK // 2
    inv_freq = (1.0 / (ROPE_BASE ** (jnp.arange(half, dtype=F32) / half))).reshape(1, half)
    base = col0 // RET_DK
    per = D_MIX // RET_DK

    def col(part):
        return pl.BlockSpec((tb, RET_DK), lambda i, h: (i, base + part * per + h))

    return pl.pallas_call(
        functools.partial(_retention_kernel, tb=tb),
        grid=(s // tb, RET_HEADS),
        in_specs=[pl.BlockSpec((1, half), lambda i, h: (0, 0)), col(0), col(1), col(2), col(3),
                  pl.BlockSpec((1, RET_DK), lambda i, h: (0, h))],
        out_specs=pl.BlockSpec((tb, RET_DK), lambda i, h: (i, h)),
        out_shape=jax.ShapeDtypeStruct((s, D_MIX), BF16),
        scratch_shapes=[pltpu.VMEM((RET_HEADS, RET_DK, RET_DK), F32), pltpu.VMEM((2, tb, RET_DK // 2), F32)],
        compiler_params=_params(("arbitrary", "arbitrary")),
        name="retention",
    )(inv_freq, u, u, u, u, gn_w.reshape(1, D_MIX))


def _causal_conv(x, w_ref, ext_ref, tail_ref):
    tb = x.shape[0]
    ext_ref[0:TAIL, :] = tail_ref[...]
    ext_ref[TAIL:, :] = x
    tail_ref[...] = x[tb - TAIL:, :]
    y = None
    for j in range(CONV_WIDTH):
        off = TAIL - (CONV_WIDTH - 1) + j
        term = w_ref[j:j + 1, :] * ext_ref[off:off + tb, :]
        y = term if y is None else y + term
    return y


def _rglru_kernel(x_ref, gate_ref, cw_ref, cb_ref, wr_ref, br_ref, wi_ref, bi_ref, a_ref, o_ref,
                  ext_ref, tail_ref, h_ref, *, tb):
    i = pl.program_id(0)

    @pl.when(i == 0)
    def _():
        h_ref[...] = jnp.zeros_like(h_ref)
        tail_ref[...] = jnp.zeros_like(tail_ref)

    xc = _causal_conv(x_ref[...], cw_ref, ext_ref, tail_ref) + cb_ref[...]
    xb = _bf(xc)
    r_parts, i_parts = [], []
    for n in range(LRU_BLOCKS):
        blk = xb[:, n * LRU_BLOCK:(n + 1) * LRU_BLOCK]
        r_parts.append(_dot(blk, wr_ref[n]))
        i_parts.append(_dot(blk, wi_ref[n]))
    r = jax.nn.sigmoid(jnp.concatenate(r_parts, axis=-1) + br_ref[...])
    ig = jax.nn.sigmoid(jnp.concatenate(i_parts, axis=-1) + bi_ref[...])
    log_a = -LRU_C * r * _softplus(-a_ref[...])
    a = jnp.exp(log_a)
    t = jnp.tanh(log_a)
    b = jnp.sqrt(-2.0 * t / (1.0 - t)) * (ig * xc)
    pos = _row_iota(a.shape) % LRU_GROUP
    s = 1
    while s < LRU_GROUP:
        keep = pos >= s
        a_prev = jnp.where(keep, pltpu.roll(a, s, 0), 1.0)
        b_prev = jnp.where(keep, pltpu.roll(b, s, 0), 0.0)
        b = a * b_prev + b
        a = a * a_prev
        s *= 2
    carry = h_ref[...]
    parts = []
    for grp in range(tb // LRU_GROUP):
        rows = slice(grp * LRU_GROUP, (grp + 1) * LRU_GROUP)
        part = a[rows] * carry + b[rows]
        parts.append(part)
        carry = part[LRU_GROUP - 1:LRU_GROUP, :]
    hh = jnp.concatenate(parts, axis=0)
    h_ref[...] = carry
    o_ref[...] = (hh * jax.nn.gelu(gate_ref[...])).astype(o_ref.dtype)


def rglru(u, col0, conv_w, conv_b, w_r, b_r, w_i, b_i, a_param, tb):
    s = u.shape[0]
    base = col0 // D_MIX
    row = lambda t: t.reshape(1, D_MIX)
    vec = pl.BlockSpec((1, D_MIX), lambda i: (0, 0))
    wblk = pl.BlockSpec((LRU_BLOCKS, LRU_BLOCK, LRU_BLOCK), lambda i: (0, 0, 0))
    return pl.pallas_call(
        functools.partial(_rglru_kernel, tb=tb),
        grid=(s // tb,),
        in_specs=[pl.BlockSpec((tb, D_MIX), lambda i: (i, base)), pl.BlockSpec((tb, D_MIX), lambda i: (i, base + 1)),
                  pl.BlockSpec((CONV_WIDTH, D_MIX), lambda i: (0, 0)), vec, wblk, vec, wblk, vec, vec],
        out_specs=pl.BlockSpec((tb, D_MIX), lambda i: (i, 0)),
        out_shape=jax.ShapeDtypeStruct((s, D_MIX), BF16),
        scratch_shapes=[pltpu.VMEM((tb + TAIL, D_MIX), F32), pltpu.VMEM((TAIL, D_MIX), F32), pltpu.VMEM((1, D_MIX), F32)],
        compiler_params=_params(("arbitrary",)),
        name="rglru",
    )(u, u, conv_w, row(conv_b), _bf(w_r), row(b_r), _bf(w_i), row(b_i), row(a_param))


def _hgrn2_kernel(q_ref, f_ref, v_ref, g_ref, lbl_ref, nw_ref, o_ref, state_ref, shift_ref, *, tb, layer):
    i = pl.program_id(0)
    h = pl.program_id(1)

    @pl.when(i == 0)
    def _():
        state_ref[h] = jnp.zeros((HG_DK, HG_DK), F32)

    logits = lbl_ref[...]
    e = jnp.exp(logits - jnp.max(logits, axis=0, keepdims=True))
    sm = e / jnp.sum(e, axis=0, keepdims=True)
    lb = jnp.zeros((1, HG_DK), F32)
    for l in range(1, layer + 1):
        lb = lb + sm[l:l + 1, :]

    f = lb + (1.0 - lb) * jax.nn.sigmoid(f_ref[...])
    cum = _chunk_cumsum_mxu(jnp.log(f))
    k = 1.0 - f
    q = _silu(q_ref[...]) * HG_DK ** -0.5
    v = v_ref[...]

    @pl.when((i == 0) & (h == 0))
    def _():
        shift_ref[:, 0:HG_SUB, :] = jnp.zeros((3, HG_SUB, HG_DK), F32)

    shift_ref[0, HG_SUB:, :] = k
    shift_ref[1, HG_SUB:, :] = v
    shift_ref[2, HG_SUB:, :] = cum
    sub_pos = _row_iota((tb, 1)) % HG_SUB
    o_diag = jnp.zeros((tb, HG_DK), F32)
    for d in range(HG_SUB):
        valid = sub_pos >= d
        k_d = k if d == 0 else shift_ref[0, HG_SUB - d:HG_SUB - d + tb, :]
        v_d = v if d == 0 else shift_ref[1, HG_SUB - d:HG_SUB - d + tb, :]
        c_d = cum if d == 0 else shift_ref[2, HG_SUB - d:HG_SUB - d + tb, :]
        w = jnp.sum(q * k_d * jnp.exp(jnp.where(valid, cum - c_d, -jnp.inf)), axis=-1, keepdims=True)
        o_diag = o_diag + w * v_d

    state = state_ref[h]
    n_sub = CHUNK // HG_SUB
    row_blk = _row_iota((CHUNK, 1)) // HG_SUB
    outs = []
    for c in range(tb // CHUNK):
        rows = slice(c * CHUNK, (c + 1) * CHUNK)
        qc, kc, vc, cc = q[rows], k[rows], _bf(v[rows]), cum[rows]
        last = cc[CHUNK - 1:CHUNK, :]
        o = _dot_nt(_bf(qc * jnp.exp(cc)), _bf(state))
        p_rows = [jnp.zeros((HG_SUB, CHUNK), F32)]
        for a in range(1, n_sub):
            sub = slice(a * HG_SUB, (a + 1) * HG_SUB)
            r_a = cc[a * HG_SUB - 1:a * HG_SUB, :]
            q_a = qc[sub] * jnp.exp(cc[sub] - r_a)
            k_a = kc * jnp.exp(jnp.where(row_blk < a, r_a - cc, -jnp.inf))
            p_rows.append(_dot_nt(_bf(q_a), _bf(k_a)))
        o = o + _dot(_bf(jnp.concatenate(p_rows, axis=0)), vc)
        state = state * jnp.exp(last) + _dot_tn(vc, _bf(kc * jnp.exp(last - cc)))
        outs.append(o)
    state_ref[h] = state
    o = jnp.concatenate(outs, axis=0) + o_diag
    o = o * lax.rsqrt(jnp.mean(o * o, axis=-1, keepdims=True) + EPS) * nw_ref[...]
    o_ref[...] = (o * _silu(g_ref[...])).astype(o_ref.dtype)


def hgrn2(u, col0, lb_logits, norm_w, layer, tb):
    s = u.shape[0]
    depth = lb_logits.shape[0]
    base = col0 // HG_DK
    per = D_MIX // HG_DK

    def col(part):
        return pl.BlockSpec((tb, HG_DK), lambda i, h: (i, base + part * per + h))

    return pl.pallas_call(
        functools.partial(_hgrn2_kernel, tb=tb, layer=layer),
        grid=(s // tb, HG_HEADS),
        in_specs=[col(0), col(1), col(2), col(3),
                  pl.BlockSpec((depth, HG_DK), lambda i, h: (0, h)), pl.BlockSpec((1, HG_DK), lambda i, h: (0, h))],
        out_specs=pl.BlockSpec((tb, HG_DK), lambda i, h: (i, h)),
        out_shape=jax.ShapeDtypeStruct((s, D_MIX), BF16),
        scratch_shapes=[pltpu.VMEM((HG_HEADS, HG_DK, HG_DK), F32), pltpu.VMEM((3, tb + HG_SUB, HG_DK), F32)],
        compiler_params=_params(("arbitrary", "arbitrary")),
        name="hgrn2",
    )(u, u, u, u, lb_logits, norm_w.reshape(1, D_MIX))


def _split(a):
    hi = _bf(a)
    return hi, _bf(a - hi.astype(F32))


def _dot_split(a, b):
    ah, al = a
    bh, bl = b
    return _dot(ah, bh) + (_dot(ah, bl) + _dot(al, bh))


def _unit_lower_solve(lows, rhss):
    ci = _row_iota(lows[0].shape)
    cj = _col_iota(lows[0].shape)
    eye = (ci == cj).astype(F32)
    on_diag_block = (ci // DN_SUB) == (cj // DN_SUB)
    neg_d = [jnp.where(on_diag_block, -low, 0.0) for low in lows]
    ps = [_split(m) for m in neg_d]
    offs = [_split(jnp.where(on_diag_block, 0.0, low)) for low in lows]
    inv_ds = [_split(eye + m) for m in neg_d]
    for _ in range(int(math.log2(DN_SUB)) - 1):
        sqs = [_dot_split(p, p) for p in ps]
        ps = [_split(sq) for sq in sqs]
        inv_ds = [_split(_dot_split(d, _split(eye + sq))) for d, sq in zip(inv_ds, sqs)]
    xs = [_dot(d[0], r) + _dot(d[1], r) for d, r in zip(inv_ds, rhss)]
    ns = [_split(-_dot_split(d, off)) for d, off in zip(inv_ds, offs)]
    for step in range(int(math.log2(CHUNK // DN_SUB))):
        xs = [x + _dot_split(n, _split(x)) for x, n in zip(xs, ns)]
        if step + 1 < int(math.log2(CHUNK // DN_SUB)):
            ns = [_split(_dot_split(n, n)) for n in ns]
    return xs


def _deltanet_kernel(q_ref, k_ref, v_ref, z_ref, ab_ref, cwq_ref, cwk_ref, cwv_ref, alog_ref, dt_ref, nw_ref, o_ref,
                     ext_ref, tail_ref, state_ref, *, tb):
    i = pl.program_id(0)
    hblk = pl.program_id(1)

    def l2n(t):
        return t * lax.rsqrt(jnp.sum(t * t, axis=-1, keepdims=True) + EPS)

    ab = ab_ref[...]
    lane = _col_iota(ab.shape)
    log_alpha = -jnp.exp(alog_ref[...]) * _softplus(ab + dt_ref[...])
    g_all = _chunk_cumsum_mxu(log_alpha)
    beta_all = jax.nn.sigmoid(ab)
    g_t = g_all.T[0:DN_HEADS, :]

    ri = _row_iota((DN_GROUP, DN_GROUP))
    ci = _col_iota((DN_GROUP, DN_GROUP))
    same_chunk = (ri // CHUNK) == (ci // CHUNK)
    incl = same_chunk & (ci <= ri)
    strict = same_chunk & (ci < ri)
    n_chunk = tb // CHUNK
    per_group = DN_GROUP // CHUNK

    h0 = hblk * DN_HPS

    @pl.when(i == 0)
    def _():
        state_ref[pl.ds(h0, DN_HPS)] = jnp.zeros((DN_HPS, DN_DK, DN_DK), F32)
        tail_ref[pl.ds(h0, DN_HPS)] = jnp.zeros((DN_HPS, 3, TAIL, DN_DK), F32)

    heads = []
    for hh in range(DN_HPS):
        h = h0 + hh
        lanes = slice(hh * DN_DK, (hh + 1) * DN_DK)

        def conv(x_ref, w_ref, part):
            return _silu(_causal_conv(x_ref[:, lanes], w_ref.at[:, lanes], ext_ref.at[hh * 3 + part], tail_ref.at[h, part]))

        q = l2n(conv(q_ref, cwq_ref, 0)) * DN_DK ** -0.5
        k = l2n(conv(k_ref, cwk_ref, 1))
        v = conv(v_ref, cwv_ref, 2)
        g = jnp.sum(jnp.where(lane == h, g_all, 0.0), axis=-1, keepdims=True)
        beta = jnp.sum(jnp.where(lane == h + DN_HEADS, beta_all, 0.0), axis=-1, keepdims=True)
        g_row = jnp.sum(jnp.where(_row_iota(g_t.shape) == h, g_t, 0.0), axis=0, keepdims=True)
        kb = _bf(k)
        qb = _bf(q)
        eg = jnp.exp(g)
        rhs = _bf(jnp.concatenate([v * beta, k * (beta * eg)], axis=-1))
        lows, rhss, qks = [], [], []
        for grp in range(tb // DN_GROUP):
            rows = slice(grp * DN_GROUP, (grp + 1) * DN_GROUP)
            decay = jnp.exp(jnp.where(incl, g[rows] - g_row[:, rows], -jnp.inf))
            lows.append(jnp.where(strict, beta[rows] * _dot_nt(kb[rows], kb[rows]) * decay, 0.0))
            rhss.append(rhs[rows])
            qks.append(_bf(_dot_nt(qb[rows], kb[rows]) * decay))
        heads.append(dict(h=h, lanes=lanes, k=k, g=g, lows=lows, rhss=rhss, qks=qks, qg=_bf(q * eg),
                          state=state_ref[h], v_parts=[], o_parts=[]))

    sols = _unit_lower_solve([m for hd in heads for m in hd["lows"]], [r for hd in heads for r in hd["rhss"]])
    for n, hd in enumerate(heads):
        sol = jnp.concatenate(sols[n * (tb // DN_GROUP):(n + 1) * (tb // DN_GROUP)], axis=0)
        hd["u"], hd["w"] = sol[:, :DN_DK], _bf(sol[:, DN_DK:])

    for c in range(n_chunk):
        rows = slice(c * CHUNK, (c + 1) * CHUNK)
        for hd in heads:
            both = _dot(jnp.concatenate([hd["w"][rows], hd["qg"][rows]], axis=0), _bf(hd["state"]))
            vb = _bf(hd["u"][rows] - both[:CHUNK])
            hd["o_parts"].append(both[CHUNK:])
            hd["v_parts"].append(vb)
            g_last = hd["g"][(c + 1) * CHUNK - 1:(c + 1) * CHUNK, :]
            k_tail = _bf(hd["k"][rows] * jnp.exp(g_last - hd["g"][rows]))
            hd["state"] = jnp.exp(g_last) * hd["state"] + _dot_tn(k_tail, vb)

    for hd in heads:
        state_ref[hd["h"]] = hd["state"]
        intra = [_dot(qk, jnp.concatenate(hd["v_parts"][n * per_group:(n + 1) * per_group], axis=0))
                 for n, qk in enumerate(hd["qks"])]
        o = jnp.concatenate(hd["o_parts"], axis=0) + jnp.concatenate(intra, axis=0)
        o = o * lax.rsqrt(jnp.mean(o * o, axis=-1, keepdims=True) + EPS) * nw_ref[...]
        o_ref[:, hd["lanes"]] = (o * _silu(z_ref[:, hd["lanes"]])).astype(o_ref.dtype)


def deltanet(u, col0, ab, conv_w, a_log, dt_bias, norm_w, tb):
    s = u.shape[0]
    wide = DN_HPS * DN_DK
    base = col0 // wide
    per = D_MIX // wide

    def col(part):
        return pl.BlockSpec((tb, wide), lambda i, h: (i, base + part * per + h))

    def cw(part):
        return pl.BlockSpec((CONV_WIDTH, wide), lambda i, h: (0, part * per + h))

    def lanes(t):
        return jnp.zeros((1, LANES), F32).at[0, :DN_HEADS].set(t)

    small = pl.BlockSpec((1, LANES), lambda i, h: (0, 0))
    return pl.pallas_call(
        functools.partial(_deltanet_kernel, tb=tb),
        grid=(s // tb, DN_HEADS // DN_HPS),
        in_specs=[col(0), col(1), col(2), col(3), pl.BlockSpec((tb, LANES), lambda i, h: (i, 0)),
                  cw(0), cw(1), cw(2), small, small, small],
        out_specs=pl.BlockSpec((tb, wide), lambda i, h: (i, h)),
        out_shape=jax.ShapeDtypeStruct((s, D_MIX), BF16),
        scratch_shapes=[pltpu.VMEM((DN_HPS * 3, tb + TAIL, DN_DK), F32), pltpu.VMEM((DN_HEADS, 3, TAIL, DN_DK), F32),
                        pltpu.VMEM((DN_HEADS, DN_DK, DN_DK), F32)],
        compiler_params=_params(("arbitrary", "arbitrary")),
        name="deltanet",
    )(u, u, u, u, ab, conv_w, conv_w, conv_w, lanes(a_log), lanes(dt_bias), norm_w.reshape(1, DN_DK))


def kernel(x, norm_mix, w_in, merge_bias, ret_gn, lru_conv_w, lru_conv_b, lru_w_r, lru_b_r, lru_w_i, lru_b_i, lru_a,
           hg_lb_logits, hg_norm, dn_conv_w, dn_a_log, dn_dt_bias, dn_norm, w_branch, w_out, norm_ffn,
           w_ffn_gate, w_ffn_up, w_ffn_down, norm_final):
    b, s, d = x.shape
    depth = w_in.shape[0]
    n_mix = 14 * D_MIX
    n_ab = 2 * DN_HEADS
    hidden = w_ffn_gate.shape[-1]
    tb = min(256, s)
    w_in_b = _bf(w_in)
    w_ab = jnp.pad(w_in_b[:, :, n_mix:n_mix + n_ab], ((0, 0), (0, 0), (0, LANES - n_ab)))
    w_gate = w_in_b[:, :, n_mix + n_ab:]
    w_branch_b, w_out_b, w_fd = _bf(w_branch), _bf(w_out), _bf(w_ffn_down)
    outs = []
    for bi in range(b):
        xs = x.reshape(s, d) if b == 1 else x[bi]
        for l in range(depth):
            hn = rmsnorm(xs, norm_mix[l], BF16)
            u = matmul(hn, w_in_b, l, n_mix, F32, 1024, 1024, "proj_mix")
            ab = matmul(hn, w_ab, l, LANES, F32, 1024, LANES, "proj_ab")
            gl = matmul(hn, w_gate, l, w_gate.shape[2], F32, 1024, 1024, "proj_gate")
            y_ret = retention(u, 0, ret_gn[l], tb)
            y_lru = rglru(u, 4 * D_MIX, lru_conv_w[l], lru_conv_b[l], lru_w_r[l], lru_b_r[l], lru_w_i[l], lru_b_i[l],
                          lru_a[l], tb)
            y_hg = hgrn2(u, 6 * D_MIX, hg_lb_logits, hg_norm[l], l, tb)
            y_dn = deltanet(u, 10 * D_MIX, ab, dn_conv_w[l], dn_a_log[l], dn_dt_bias[l], dn_norm[l], tb)
            merged = merge(gl, merge_bias[l], (y_ret, y_lru, y_hg, y_dn), w_branch_b, l, 1024, 512)
            xs = matmul_residual(merged, w_out_b, l, xs, 1024, 1024, d, "out_proj")
            h2 = rmsnorm(xs, norm_ffn[l], BF16)
            act = ffn_up(h2, w_ffn_gate, w_ffn_up, l, 2048, 256)
            xs = matmul_residual(act, w_fd, l, xs, 1024, 512, hidden // 2, "ffn_down")
        outs.append(rmsnorm(xs, norm_final, F32))
    return outs[0].reshape(1, s, d) if b == 1 else jnp.stack(outs, axis=0)
```

```python
import functools
import math

import jax
import jax.numpy as jnp
from jax import lax
from jax.experimental import pallas as pl
from jax.experimental.pallas import tpu as pltpu

F32 = jnp.float32
BF16 = jnp.bfloat16

CHUNK = 64
HG_SUB = 16
DN_SUB = 16
D_MIX = 1024
RET_HEADS = 4
RET_DK = 256
ROPE_BASE = 10000.0
LRU_BLOCKS = 8
LRU_BLOCK = 128
LRU_C = 8.0
LRU_GROUP = 8
CONV_WIDTH = 4
HG_HEADS = 8
HG_DK = 128
DN_HEADS = 8
DN_DK = 128
DN_HPS = 4
DN_GROUP = 2 * CHUNK
EPS = 1e-6
LANES = 128
TAIL = 8

V7X_VMEM_LIMIT = 56 * 1024 * 1024

_RET_LOG_GAMMA = tuple(math.log(1.0 - 2.0 ** (-5.0 - h)) for h in range(RET_HEADS))

NT = (((1,), (1,)), ((), ()))
TN = (((0,), (0,)), ((), ()))


def _dot(a, b):
    return jnp.dot(a, b, preferred_element_type=F32)


def _dot_nt(a, b):
    return lax.dot_general(a, b, NT, preferred_element_type=F32)


def _dot_tn(a, b):
    return lax.dot_general(a, b, TN, preferred_element_type=F32)


def _bf(x):
    return x.astype(BF16)


def _silu(x):
    return x * jax.nn.sigmoid(x)


def _softplus(x):
    return jnp.maximum(x, 0.0) + jnp.log(1.0 + jnp.exp(-jnp.abs(x)))


def _params(sem, vmem=None):
    return pltpu.CompilerParams(dimension_semantics=sem, vmem_limit_bytes=vmem)


def _row_iota(shape):
    return lax.broadcasted_iota(jnp.int32, shape, 0)


def _col_iota(shape):
    return lax.broadcasted_iota(jnp.int32, shape, 1)


def _chunk_cumsum_mxu(x):
    n = x.shape[0]
    ri = _row_iota((n, n))
    ci = _col_iota((n, n))
    tri = (((ri // CHUNK) == (ci // CHUNK)) & (ci <= ri)).astype(BF16)
    hi = _bf(x)
    r1 = x - hi.astype(F32)
    mid = _bf(r1)
    lo = _bf(r1 - mid.astype(F32))
    return _dot(tri, hi) + (_dot(tri, mid) + _dot(tri, lo))


def _rmsnorm_kernel(x_ref, g_ref, o_ref):
    x = x_ref[...]
    y = x * lax.rsqrt(jnp.mean(x * x, axis=-1, keepdims=True) + EPS)
    o_ref[...] = (y * g_ref[...]).astype(o_ref.dtype)


def rmsnorm(x, g, out_dtype, tb=512):
    s, d = x.shape
    tb = min(tb, s)
    return pl.pallas_call(
        _rmsnorm_kernel,
        grid=(s // tb,),
        in_specs=[pl.BlockSpec((tb, d), lambda i: (i, 0)), pl.BlockSpec((1, d), lambda i: (0, 0))],
        out_specs=pl.BlockSpec((tb, d), lambda i: (i, 0)),
        out_shape=jax.ShapeDtypeStruct((s, d), out_dtype),
        compiler_params=_params(("arbitrary",), V7X_VMEM_LIMIT),
        name="rmsnorm",
    )(x, g.reshape(1, d))


def _cast_kernel(x_ref, o_ref):
    o_ref[...] = x_ref[...].astype(o_ref.dtype)


def cast_bf16(w, rows):
    nl, k, n = w.shape
    return pl.pallas_call(
        _cast_kernel,
        grid=(nl, k // rows),
        in_specs=[pl.BlockSpec((None, rows, n), lambda l, i: (l, i, 0))],
        out_specs=pl.BlockSpec((None, rows, n), lambda l, i: (l, i, 0)),
        out_shape=jax.ShapeDtypeStruct(w.shape, BF16),
        compiler_params=_params(("arbitrary", "arbitrary"), V7X_VMEM_LIMIT),
        name="cast_bf16",
    )(w)


def _mm_kernel(a_ref, b_ref, o_ref):
    o_ref[...] = _dot(a_ref[...], _bf(b_ref[...])).astype(o_ref.dtype)


def matmul(a, w, layer, n, out_dtype, tm, tn, name):
    m, k = a.shape
    tm, tn = min(tm, m), min(tn, n)
    return pl.pallas_call(
        _mm_kernel,
        grid=(m // tm, n // tn),
        in_specs=[pl.BlockSpec((tm, k), lambda i, j: (i, 0)), pl.BlockSpec((None, k, tn), lambda i, j: (layer, 0, j))],
        out_specs=pl.BlockSpec((tm, tn), lambda i, j: (i, j)),
        out_shape=jax.ShapeDtypeStruct((m, n), out_dtype),
        compiler_params=_params(("arbitrary", "arbitrary"), V7X_VMEM_LIMIT),
        name=name,
    )(a, w)


def _mm_res_kernel(a_ref, b_ref, r_ref, o_ref):
    @pl.when(pl.program_id(2) == 0)
    def _():
        o_ref[...] = r_ref[...]

    o_ref[...] += _dot(a_ref[...], b_ref[...])


def matmul_residual(a, w, layer, res, tm, tn, tk, name):
    m, k = a.shape
    n = w.shape[2]
    tm, tn = min(tm, m), min(tn, n)
    return pl.pallas_call(
        _mm_res_kernel,
        grid=(m // tm, n // tn, k // tk),
        in_specs=[pl.BlockSpec((tm, tk), lambda i, j, kk: (i, kk)),
                  pl.BlockSpec((None, tk, tn), lambda i, j, kk: (layer, kk, j)),
                  pl.BlockSpec((tm, tn), lambda i, j, kk: (i, j))],
        out_specs=pl.BlockSpec((tm, tn), lambda i, j, kk: (i, j)),
        out_shape=jax.ShapeDtypeStruct((m, n), F32),
        compiler_params=_params(("arbitrary", "arbitrary", "arbitrary"), V7X_VMEM_LIMIT),
        name=name,
    )(a, w, res)


def _ffn_up_kernel(h_ref, wg_ref, wu_ref, o_ref):
    h = h_ref[...]
    o_ref[...] = (_silu(_dot(h, _bf(wg_ref[...]))) * _dot(h, _bf(wu_ref[...]))).astype(o_ref.dtype)


def ffn_up(h, wg, wu, layer, tm, tn):
    m, k = h.shape
    n = wg.shape[2]
    tm = min(tm, m)
    return pl.pallas_call(
        _ffn_up_kernel,
        grid=(m // tm, n // tn),
        in_specs=[pl.BlockSpec((tm, k), lambda i, j: (i, 0)),
                  pl.BlockSpec((None, k, tn), lambda i, j: (layer, 0, j)),
                  pl.BlockSpec((None, k, tn), lambda i, j: (layer, 0, j))],
        out_specs=pl.BlockSpec((tm, tn), lambda i, j: (i, j)),
        out_shape=jax.ShapeDtypeStruct((m, n), BF16),
        compiler_params=_params(("arbitrary", "arbitrary"), V7X_VMEM_LIMIT),
        name="ffn_up",
    )(h, wg, wu)


def _merge_kernel(g0, g1, g2, g3, b_ref, y0, y1, y2, y3, w_ref, o_ref):
    acc = None
    for n, (g_ref, y_ref) in enumerate(((g0, y0), (g1, y1), (g2, y2), (g3, y3))):
        gate = jax.nn.sigmoid(g_ref[...] + b_ref[n:n + 1, :])
        term = gate * _dot(y_ref[...], w_ref[n])
        acc = term if acc is None else acc + term
    o_ref[...] = acc.astype(o_ref.dtype)


def merge(gl, bias, ys, w_branch, layer, tm, tn):
    s = gl.shape[0]
    _, nb, dmix, d = w_branch.shape
    tm = min(tm, s)
    nj = d // tn
    g_specs = [pl.BlockSpec((tm, tn), functools.partial(lambda i, j, n: (i, n * nj + j), n=n)) for n in range(nb)]
    y_specs = [pl.BlockSpec((tm, dmix), lambda i, j: (i, 0)) for _ in range(nb)]
    return pl.pallas_call(
        _merge_kernel,
        grid=(s // tm, nj),
        in_specs=g_specs + [pl.BlockSpec((nb, tn), lambda i, j: (0, j))] + y_specs
        + [pl.BlockSpec((None, nb, dmix, tn), lambda i, j: (layer, 0, 0, j))],
        out_specs=pl.BlockSpec((tm, tn), lambda i, j: (i, j)),
        out_shape=jax.ShapeDtypeStruct((s, d), BF16),
        compiler_params=_params(("arbitrary", "arbitrary"), V7X_VMEM_LIMIT),
        name="merge",
    )(gl, gl, gl, gl, bias.reshape(nb, d), *ys, w_branch)


def _retention_kernel(inv_ref, q_ref, k_ref, v_ref, g_ref, gn_ref, o_ref, state_ref, trig_ref, *, tb):
    i = pl.program_id(0)
    h = pl.program_id(1)

    @pl.when(i == 0)
    def _():
        state_ref[h] = jnp.zeros((RET_DK, RET_DK), F32)

    half = RET_DK // 2

    @pl.when(h == 0)
    def _():
        pos = (i * tb + _row_iota((tb, 1))).astype(F32)
        ang = pos * inv_ref[...]
        trig_ref[0] = jnp.cos(ang)
        trig_ref[1] = jnp.sin(ang)

    cos, sin = trig_ref[0], trig_ref[1]

    def rotary(t):
        t1, t2 = t[:, :half], t[:, half:]
        return jnp.concatenate([t1 * cos - t2 * sin, t1 * sin + t2 * cos], axis=-1)

    q = rotary(q_ref[...]) * RET_DK ** -0.5
    k = rotary(k_ref[...])
    v = v_ref[...]

    log_gamma = jnp.float32(_RET_LOG_GAMMA[0])
    for hh in range(1, RET_HEADS):
        log_gamma = jnp.where(h == hh, jnp.float32(_RET_LOG_GAMMA[hh]), log_gamma)
    ci = _row_iota((CHUNK, CHUNK))
    cj = _col_iota((CHUNK, CHUNK))
    intra = jnp.exp(log_gamma * jnp.abs(ci - cj).astype(F32))
    cpos = _row_iota((CHUNK, 1)).astype(F32)
    q_decay = jnp.exp(log_gamma * cpos)
    k_decay = jnp.exp(log_gamma * (CHUNK - cpos))
    chunk_decay = jnp.exp(jnp.full((1, 1), log_gamma * CHUNK, F32))

    state = state_ref[h]
    outs = []
    for c in range(tb // CHUNK):
        rows = slice(c * CHUNK, (c + 1) * CHUNK)
        qc, kc, vc = q[rows], k[rows], _bf(v[rows])
        scores = _dot_nt(_bf(qc), _bf(kc)) * intra
        o = _dot(_bf(scores), vc) + _dot(_bf(qc * q_decay), _bf(state))
        state = chunk_decay * state + _dot_tn(_bf(kc * k_decay), vc)
        outs.append(o)
    state_ref[h] = state
    o = jnp.concatenate(outs, axis=0)
    mu = jnp.mean(o, axis=-1, keepdims=True)
    var = jnp.mean(jnp.square(o - mu), axis=-1, keepdims=True)
    o = (o - mu) * lax.rsqrt(var + EPS) * gn_ref[...]
    o_ref[...] = (_silu(g_ref[...]) * o).astype(o_ref.dtype)


def retention(u, col0, gn_w, tb):
    s = u.shape[0]
    half = RET_DK // 2
    inv_freq = (1.0 / (ROPE_BASE ** (jnp.arange(half, dtype=F32) / half))).reshape(1, half)
    base = col0 // RET_DK
    per = D_MIX // RET_DK

    def col(part):
        return pl.BlockSpec((tb, RET_DK), lambda i, h: (i, base + part * per + h))

    return pl.pallas_call(
        functools.partial(_retention_kernel, tb=tb),
        grid=(s // tb, RET_HEADS),
        in_specs=[pl.BlockSpec((1, half), lambda i, h: (0, 0)), col(0), col(1), col(2), col(3),
                  pl.BlockSpec((1, RET_DK), lambda i, h: (0, h))],
        out_specs=pl.BlockSpec((tb, RET_DK), lambda i, h: (i, h)),
        out_shape=jax.ShapeDtypeStruct((s, D_MIX), BF16),
        scratch_shapes=[pltpu.VMEM((RET_HEADS, RET_DK, RET_DK), F32), pltpu.VMEM((2, tb, RET_DK // 2), F32)],
        compiler_params=_params(("arbitrary", "arbitrary")),
        name="retention",
    )(inv_freq, u, u, u, u, gn_w.reshape(1, D_MIX))


def _causal_conv(x, w_ref, ext_ref, tail_ref):
    tb = x.shape[0]
    ext_ref[0:TAIL, :] = tail_ref[...]
    ext_ref[TAIL:, :] = x
    tail_ref[...] = x[tb - TAIL:, :]
    y = None
    for j in range(CONV_WIDTH):
        off = TAIL - (CONV_WIDTH - 1) + j
        term = w_ref[j:j + 1, :] * ext_ref[off:off + tb, :]
        y = term if y is None else y + term
    return y


def _rglru_kernel(x_ref, gate_ref, cw_ref, cb_ref, wr_ref, br_ref, wi_ref, bi_ref, a_ref, o_ref,
                  ext_ref, tail_ref, h_ref, *, tb):
    i = pl.program_id(0)

    @pl.when(i == 0)
    def _():
        h_ref[...] = jnp.zeros_like(h_ref)
        tail_ref[...] = jnp.zeros_like(tail_ref)

    xc = _causal_conv(x_ref[...], cw_ref, ext_ref, tail_ref) + cb_ref[...]
    xb = _bf(xc)
    r_parts, i_parts = [], []
    for n in range(LRU_BLOCKS):
        blk = xb[:, n * LRU_BLOCK:(n + 1) * LRU_BLOCK]
        r_parts.append(_dot(blk, wr_ref[n]))
        i_parts.append(_dot(blk, wi_ref[n]))
    r = jax.nn.sigmoid(jnp.concatenate(r_parts, axis=-1) + br_ref[...])
    ig = jax.nn.sigmoid(jnp.concatenate(i_parts, axis=-1) + bi_ref[...])
    log_a = -LRU_C * r * _softplus(-a_ref[...])
    a = jnp.exp(log_a)
    t = jnp.tanh(log_a)
    b = jnp.sqrt(-2.0 * t / (1.0 - t)) * (ig * xc)
    pos = _row_iota(a.shape) % LRU_GROUP
    s = 1
    while s < LRU_GROUP:
        keep = pos >= s
        a_prev = jnp.where(keep, pltpu.roll(a, s, 0), 1.0)
        b_prev = jnp.where(keep, pltpu.roll(b, s, 0), 0.0)
        b = a * b_prev + b
        a = a * a_prev
        s *= 2
    carry = h_ref[...]
    parts = []
    for grp in range(tb // LRU_GROUP):
        rows = slice(grp * LRU_GROUP, (grp + 1) * LRU_GROUP)
        part = a[rows] * carry + b[rows]
        parts.append(part)
        carry = part[LRU_GROUP - 1:LRU_GROUP, :]
    hh = jnp.concatenate(parts, axis=0)
    h_ref[...] = carry
    o_ref[...] = (hh * jax.nn.gelu(gate_ref[...])).astype(o_ref.dtype)


def rglru(u, col0, conv_w, conv_b, w_r, b_r, w_i, b_i, a_param, tb):
    s = u.shape[0]
    base = col0 // D_MIX
    row = lambda t: t.reshape(1, D_MIX)
    vec = pl.BlockSpec((1, D_MIX), lambda i: (0, 0))
    wblk = pl.BlockSpec((LRU_BLOCKS, LRU_BLOCK, LRU_BLOCK), lambda i: (0, 0, 0))
    return pl.pallas_call(
        functools.partial(_rglru_kernel, tb=tb),
        grid=(s // tb,),
        in_specs=[pl.BlockSpec((tb, D_MIX), lambda i: (i, base)), pl.BlockSpec((tb, D_MIX), lambda i: (i, base + 1)),
                  pl.BlockSpec((CONV_WIDTH, D_MIX), lambda i: (0, 0)), vec, wblk, vec, wblk, vec, vec],
        out_specs=pl.BlockSpec((tb, D_MIX), lambda i: (i, 0)),
        out_shape=jax.ShapeDtypeStruct((s, D_MIX), BF16),
        scratch_shapes=[pltpu.VMEM((tb + TAIL, D_MIX), F32), pltpu.VMEM((TAIL, D_MIX), F32), pltpu.VMEM((1, D_MIX), F32)],
        compiler_params=_params(("arbitrary",)),
        name="rglru",
    )(u, u, conv_w, row(conv_b), _bf(w_r), row(b_r), _bf(w_i), row(b_i), row(a_param))


def _hgrn2_kernel(q_ref, f_ref, v_ref, g_ref, lbl_ref, nw_ref, o_ref, state_ref, shift_ref, *, tb, layer):
    i = pl.program_id(0)
    h = pl.program_id(1)

    @pl.when(i == 0)
    def _():
        state_ref[h] = jnp.zeros((HG_DK, HG_DK), F32)

    logits = lbl_ref[...]
    e = jnp.exp(logits - jnp.max(logits, axis=0, keepdims=True))
    sm = e / jnp.sum(e, axis=0, keepdims=True)
    lb = jnp.zeros((1, HG_DK), F32)
    for l in range(1, layer + 1):
        lb = lb + sm[l:l + 1, :]

    f = lb + (1.0 - lb) * jax.nn.sigmoid(f_ref[...])
    cum = _chunk_cumsum_mxu(jnp.log(f))
    k = 1.0 - f
    q = _silu(q_ref[...]) * HG_DK ** -0.5
    v = v_ref[...]

    @pl.when((i == 0) & (h == 0))
    def _():
        shift_ref[:, 0:HG_SUB, :] = jnp.zeros((3, HG_SUB, HG_DK), F32)

    shift_ref[0, HG_SUB:, :] = k
    shift_ref[1, HG_SUB:, :] = v
    shift_ref[2, HG_SUB:, :] = cum
    sub_pos = _row_iota((tb, 1)) % HG_SUB
    o_diag = jnp.zeros((tb, HG_DK), F32)
    for d in range(HG_SUB):
        valid = sub_pos >= d
        k_d = k if d == 0 else shift_ref[0, HG_SUB - d:HG_SUB - d + tb, :]
        v_d = v if d == 0 else shift_ref[1, HG_SUB - d:HG_SUB - d + tb, :]
        c_d = cum if d == 0 else shift_ref[2, HG_SUB - d:HG_SUB - d + tb, :]
        w = jnp.sum(q * k_d * jnp.exp(jnp.where(valid, cum - c_d, -jnp.inf)), axis=-1, keepdims=True)
        o_diag = o_diag + w * v_d

    state = state_ref[h]
    n_sub = CHUNK // HG_SUB
    row_blk = _row_iota((CHUNK, 1)) // HG_SUB
    outs = []
    for c in range(tb // CHUNK):
        rows = slice(c * CHUNK, (c + 1) * CHUNK)
        qc, kc, vc, cc = q[rows], k[rows], _bf(v[rows]), cum[rows]
        last = cc[CHUNK - 1:CHUNK, :]
        o = _dot_nt(_bf(qc * jnp.exp(cc)), _bf(state))
        p_rows = [jnp.zeros((HG_SUB, CHUNK), F32)]
        for a in range(1, n_sub):
            sub = slice(a * HG_SUB, (a + 1) * HG_SUB)
            r_a = cc[a * HG_SUB - 1:a * HG_SUB, :]
            q_a = qc[sub] * jnp.exp(cc[sub] - r_a)
            k_a = kc * jnp.exp(jnp.where(row_blk < a, r_a - cc, -jnp.inf))
            p_rows.append(_dot_nt(_bf(q_a), _bf(k_a)))
        o = o + _dot(_bf(jnp.concatenate(p_rows, axis=0)), vc)
        state = state * jnp.exp(last) + _dot_tn(vc, _bf(kc * jnp.exp(last - cc)))
        outs.append(o)
    state_ref[h] = state
    o = jnp.concatenate(outs, axis=0) + o_diag
    o = o * lax.rsqrt(jnp.mean(o * o, axis=-1, keepdims=True) + EPS) * nw_ref[...]
    o_ref[...] = (o * _silu(g_ref[...])).astype(o_ref.dtype)


def hgrn2(u, col0, lb_logits, norm_w, layer, tb):
    s = u.shape[0]
    depth = lb_logits.shape[0]
    base = col0 // HG_DK
    per = D_MIX // HG_DK

    def col(part):
        return pl.BlockSpec((tb, HG_DK), lambda i, h: (i, base + part * per + h))

    return pl.pallas_call(
        functools.partial(_hgrn2_kernel, tb=tb, layer=layer),
        grid=(s // tb, HG_HEADS),
        in_specs=[col(0), col(1), col(2), col(3),
                  pl.BlockSpec((depth, HG_DK), lambda i, h: (0, h)), pl.BlockSpec((1, HG_DK), lambda i, h: (0, h))],
        out_specs=pl.BlockSpec((tb, HG_DK), lambda i, h: (i, h)),
        out_shape=jax.ShapeDtypeStruct((s, D_MIX), BF16),
        scratch_shapes=[pltpu.VMEM((HG_HEADS, HG_DK, HG_DK), F32), pltpu.VMEM((3, tb + HG_SUB, HG_DK), F32)],
        compiler_params=_params(("arbitrary", "arbitrary")),
        name="hgrn2",
    )(u, u, u, u, lb_logits, norm_w.reshape(1, D_MIX))


def _split(a):
    hi = _bf(a)
    return hi, _bf(a - hi.astype(F32))


def _dot_split(a, b):
    ah, al = a
    bh, bl = b
    return _dot(ah, bh) + (_dot(ah, bl) + _dot(al, bh))


def _unit_lower_solve(lows, rhss):
    ci = _row_iota(lows[0].shape)
    cj = _col_iota(lows[0].shape)
    eye = (ci == cj).astype(F32)
    on_diag_block = (ci // DN_SUB) == (cj // DN_SUB)
    neg_d = [jnp.where(on_diag_block, -low, 0.0) for low in lows]
    ps = [_split(m) for m in neg_d]
    offs = [_split(jnp.where(on_diag_block, 0.0, low)) for low in lows]
    inv_ds = [_split(eye + m) for m in neg_d]
    for _ in range(int(math.log2(DN_SUB)) - 1):
        sqs = [_dot_split(p, p) for p in ps]
        ps = [_split(sq) for sq in sqs]
        inv_ds = [_split(_dot_split(d, _split(eye + sq))) for d, sq in zip(inv_ds, sqs)]
    xs = [_dot(d[0], r) + _dot(d[1], r) for d, r in zip(inv_ds, rhss)]
    ns = [_split(-_dot_split(d, off)) for d, off in zip(inv_ds, offs)]
    for step in range(int(math.log2(CHUNK // DN_SUB))):
        xs = [x + _dot_split(n, _split(x)) for x, n in zip(xs, ns)]
        if step + 1 < int(math.log2(CHUNK // DN_SUB)):
            ns = [_split(_dot_split(n, n)) for n in ns]
    return xs


def _deltanet_kernel(q_ref, k_ref, v_ref, z_ref, ab_ref, cwq_ref, cwk_ref, cwv_ref, alog_ref, dt_ref, nw_ref, o_ref,
                     ext_ref, tail_ref, state_ref, *, tb):
    i = pl.program_id(0)
    hblk = pl.program_id(1)

    def l2n(t):
        return t * lax.rsqrt(jnp.sum(t * t, axis=-1, keepdims=True) + EPS)

    ab = ab_ref[...]
    lane = _col_iota(ab.shape)
    log_alpha = -jnp.exp(alog_ref[...]) * _softplus(ab + dt_ref[...])
    g_all = _chunk_cumsum_mxu(log_alpha)
    beta_all = jax.nn.sigmoid(ab)
    g_t = g_all.T[0:DN_HEADS, :]

    ri = _row_iota((DN_GROUP, DN_GROUP))
    ci = _col_iota((DN_GROUP, DN_GROUP))
    same_chunk = (ri // CHUNK) == (ci // CHUNK)
    incl = same_chunk & (ci <= ri)
    strict = same_chunk & (ci < ri)
    n_chunk = tb // CHUNK
    per_group = DN_GROUP // CHUNK

    h0 = hblk * DN_HPS

    @pl.when(i == 0)
    def _():
        state_ref[pl.ds(h0, DN_HPS)] = jnp.zeros((DN_HPS, DN_DK, DN_DK), F32)
        tail_ref[pl.ds(h0, DN_HPS)] = jnp.zeros((DN_HPS, 3, TAIL, DN_DK), F32)

    heads = []
    for hh in range(DN_HPS):
        h = h0 + hh
        lanes = slice(hh * DN_DK, (hh + 1) * DN_DK)

        def conv(x_ref, w_ref, part):
            return _silu(_causal_conv(x_ref[:, lanes], w_ref.at[:, lanes], ext_ref.at[hh * 3 + part], tail_ref.at[h, part]))

        q = l2n(conv(q_ref, cwq_ref, 0)) * DN_DK ** -0.5
        k = l2n(conv(k_ref, cwk_ref, 1))
        v = conv(v_ref, cwv_ref, 2)
        g = jnp.sum(jnp.where(lane == h, g_all, 0.0), axis=-1, keepdims=True)
        beta = jnp.sum(jnp.where(lane == h + DN_HEADS, beta_all, 0.0), axis=-1, keepdims=True)
        g_row = jnp.sum(jnp.where(_row_iota(g_t.shape) == h, g_t, 0.0), axis=0, keepdims=True)
        kb = _bf(k)
        qb = _bf(q)
        eg = jnp.exp(g)
        rhs = _bf(jnp.concatenate([v * beta, k * (beta * eg)], axis=-1))
        lows, rhss, qks = [], [], []
        for grp in range(tb // DN_GROUP):
            rows = slice(grp * DN_GROUP, (grp + 1) * DN_GROUP)
            decay = jnp.exp(jnp.where(incl, g[rows] - g_row[:, rows], -jnp.inf))
            lows.append(jnp.where(strict, beta[rows] * _dot_nt(kb[rows], kb[rows]) * decay, 0.0))
            rhss.append(rhs[rows])
            qks.append(_bf(_dot_nt(qb[rows], kb[rows]) * decay))
        heads.append(dict(h=h, lanes=lanes, k=k, g=g, lows=lows, rhss=rhss, qks=qks, qg=_bf(q * eg),
                          state=state_ref[h], v_parts=[], o_parts=[]))

    sols = _unit_lower_solve([m for hd in heads for m in hd["lows"]], [r for hd in heads for r in hd["rhss"]])
    for n, hd in enumerate(heads):
        sol = jnp.concatenate(sols[n * (tb // DN_GROUP):(n + 1) * (tb // DN_GROUP)], axis=0)
        hd["u"], hd["w"] = sol[:, :DN_DK], _bf(sol[:, DN_DK:])

    for c in range(n_chunk):
        rows = slice(c * CHUNK, (c + 1) * CHUNK)
        for hd in heads:
            both = _dot(jnp.concatenate([hd["w"][rows], hd["qg"][rows]], axis=0), _bf(hd["state"]))
            vb = _bf(hd["u"][rows] - both[:CHUNK])
            hd["o_parts"].append(both[CHUNK:])
            hd["v_parts"].append(vb)
            g_last = hd["g"][(c + 1) * CHUNK - 1:(c + 1) * CHUNK, :]
            k_tail = _bf(hd["k"][rows] * jnp.exp(g_last - hd["g"][rows]))
            hd["state"] = jnp.exp(g_last) * hd["state"] + _dot_tn(k_tail, vb)

    for hd in heads:
        state_ref[hd["h"]] = hd["state"]
        intra = [_dot(qk, jnp.concatenate(hd["v_parts"][n * per_group:(n + 1) * per_group], axis=0))
                 for n, qk in enumerate(hd["qks"])]
        o = jnp.concatenate(hd["o_parts"], axis=0) + jnp.concatenate(intra, axis=0)
        o = o * lax.rsqrt(jnp.mean(o * o, axis=-1, keepdims=True) + EPS) * nw_ref[...]
        o_ref[:, hd["lanes"]] = (o * _silu(z_ref[:, hd["lanes"]])).astype(o_ref.dtype)


def deltanet(u, col0, ab, conv_w, a_log, dt_bias, norm_w, tb):
    s = u.shape[0]
    wide = DN_HPS * DN_DK
    base = col0 // wide
    per = D_MIX // wide

    def col(part):
        return pl.BlockSpec((tb, wide), lambda i, h: (i, base + part * per + h))

    def cw(part):
        return pl.BlockSpec((CONV_WIDTH, wide), lambda i, h: (0, part * per + h))

    def lanes(t):
        return jnp.zeros((1, LANES), F32).at[0, :DN_HEADS].set(t)

    small = pl.BlockSpec((1, LANES), lambda i, h: (0, 0))
    return pl.pallas_call(
        functools.partial(_deltanet_kernel, tb=tb),
        grid=(s // tb, DN_HEADS // DN_HPS),
        in_specs=[col(0), col(1), col(2), col(3), pl.BlockSpec((tb, LANES), lambda i, h: (i, 0)),
                  cw(0), cw(1), cw(2), small, small, small],
        out_specs=pl.BlockSpec((tb, wide), lambda i, h: (i, h)),
        out_shape=jax.ShapeDtypeStruct((s, D_MIX), BF16),
        scratch_shapes=[pltpu.VMEM((DN_HPS * 3, tb + TAIL, DN_DK), F32), pltpu.VMEM((DN_HEADS, 3, TAIL, DN_DK), F32),
                        pltpu.VMEM((DN_HEADS, DN_DK, DN_DK), F32)],
        compiler_params=_params(("arbitrary", "arbitrary")),
        name="deltanet",
    )(u, u, u, u, ab, conv_w, conv_w, conv_w, lanes(a_log), lanes(dt_bias), norm_w.reshape(1, DN_DK))


def kernel(x, norm_mix, w_in, merge_bias, ret_gn, lru_conv_w, lru_conv_b, lru_w_r, lru_b_r, lru_w_i, lru_b_i, lru_a,
           hg_lb_logits, hg_norm, dn_conv_w, dn_a_log, dn_dt_bias, dn_norm, w_branch, w_out, norm_ffn,
           w_ffn_gate, w_ffn_up, w_ffn_down, norm_final):
    b, s, d = x.shape
    depth = w_in.shape[0]
    n_mix = 14 * D_MIX
    n_ab = 2 * DN_HEADS
    hidden = w_ffn_gate.shape[-1]
    tb = min(256, s)
    w_in_b = cast_bf16(w_in, 128)
    w_ab = jnp.pad(w_in_b[:, :, n_mix:n_mix + n_ab], ((0, 0), (0, 0), (0, LANES - n_ab)))
    w_gate = w_in_b[:, :, n_mix + n_ab:]
    w_branch_b = cast_bf16(w_branch.reshape(depth, -1, d), 512).reshape(w_branch.shape)
    w_out_b = cast_bf16(w_out, 512)
    w_fd = cast_bf16(w_ffn_down, 256)
    outs = []
    for bi in range(b):
        xs = x.reshape(s, d) if b == 1 else x[bi]
        for l in range(depth):
            hn = rmsnorm(xs, norm_mix[l], BF16)
            u = matmul(hn, w_in_b, l, n_mix, F32, 1024, 1024, "proj_mix")
            ab = matmul(hn, w_ab, l, LANES, F32, 1024, LANES, "proj_ab")
            gl = matmul(hn, w_gate, l, w_gate.shape[2], F32, 1024, 1024, "proj_gate")
            y_ret = retention(u, 0, ret_gn[l], tb)
            y_lru = rglru(u, 4 * D_MIX, lru_conv_w[l], lru_conv_b[l], lru_w_r[l], lru_b_r[l], lru_w_i[l], lru_b_i[l],
                          lru_a[l], tb)
            y_hg = hgrn2(u, 6 * D_MIX, hg_lb_logits, hg_norm[l], l, tb)
            y_dn = deltanet(u, 10 * D_MIX, ab, dn_conv_w[l], dn_a_log[l], dn_dt_bias[l], dn_norm[l], tb)
            merged = merge(gl, merge_bias[l], (y_ret, y_lru, y_hg, y_dn), w_branch_b, l, 1024, 512)
            xs = matmul_residual(merged, w_out_b, l, xs, 1024, 1024, d, "out_proj")
            h2 = rmsnorm(xs, norm_ffn[l], BF16)
            act = ffn_up(h2, w_ffn_gate, w_ffn_up, l, 2048, 256)
            xs = matmul_residual(act, w_fd, l, xs, 1024, 512, hidden // 2, "ffn_down")
        outs.append(rmsnorm(xs, norm_final, F32))
    return outs[0].reshape(1, s, d) if b == 1 else jnp.stack(outs, axis=0)
```
